```python
import jax, jax.numpy as jnp
from jax import lax
import numpy as np

D_MODEL = 2048
BATCH = 4
SEQ = 8192
DEPTH = 2
DEC_BATCH = 8
DEC_SEQ = 16
PAST_LEN = 2048

CHUNK = 64
EPS = 1e-6
ATTN_HEADS = 16
ATTN_KV_HEADS = 2
ATTN_GROUP = ATTN_HEADS // ATTN_KV_HEADS
HEAD_DIM = 64
WINDOW = 128
WINDOW_CHUNKS = WINDOW // CHUNK
ATTN_Q_DIM = ATTN_HEADS * HEAD_DIM
ATTN_KV_DIM = ATTN_KV_HEADS * HEAD_DIM
SSD_HEADS = 16
SSD_HEAD_DIM = 64
SSD_INNER = SSD_HEADS * SSD_HEAD_DIM
SSD_STATE = 128
SSD_GROUPS = 2
SSD_HPG = SSD_HEADS // SSD_GROUPS
SSD_CONV = 4
SSD_CHUNK = 64
SSD_CONV_DIM = SSD_INNER + 2 * SSD_GROUPS * SSD_STATE
IN0_DIM = ATTN_Q_DIM + 2 * ATTN_KV_DIM + SSD_INNER + SSD_CONV_DIM + SSD_HEADS
MIX0_DIM = ATTN_Q_DIM + SSD_INNER
SCONV_WIDTH = 3
D_FF = 5632
N_MOD = 9
N_EVEN = (DEPTH + 1) // 2
N_ODD = DEPTH // 2

kernel_name = "hybrid_streaming_swa_ssd_shortconv_step"


def rmsnorm(x, g):
    x32 = x.astype(jnp.float32)
    y = x32 * lax.rsqrt(jnp.mean(x32 * x32, axis=-1, keepdims=True) + EPS)
    return (y * g.astype(jnp.float32)).astype(x.dtype)


def swiglu(h, wg, wu, wd):
    return (jax.nn.silu(h @ wg) * (h @ wu)) @ wd


def causal_dwconv(x, prev, w):
    k = w.shape[0]
    length = x.shape[1]
    xp = jnp.concatenate([prev.astype(x.dtype), x], axis=1)
    y = xp[:, 0:length] * w[0]
    for i in range(1, k):
        y = y + xp[:, i:i + length] * w[i]
    return y, xp[:, -(k - 1):]


def alibi_slopes():
    return jnp.asarray(2.0 ** (-8.0 * np.arange(1, ATTN_HEADS + 1) / ATTN_HEADS), jnp.float32)


def sink_attention(q, k, v, dist, valid, sinks):
    slopes = alibi_slopes().reshape(ATTN_KV_HEADS, ATTN_GROUP)
    s = jnp.einsum('bnqhgd,bnshd->bnhgqs', q, k).astype(jnp.float32) * (HEAD_DIM ** -0.5)
    s = s - slopes[:, :, None, None] * dist[None, :, None, None]
    s = jnp.where(valid[None, :, None, None], s, -jnp.inf)
    sink = jnp.broadcast_to(sinks.astype(jnp.float32).reshape(ATTN_KV_HEADS, ATTN_GROUP)[:, :, None, None],
                            s.shape[:-1] + (1,))
    p = jax.nn.softmax(jnp.concatenate([s, sink], axis=-1), axis=-1)[..., :-1]
    return jnp.einsum('bnhgqs,bnshd->bnqhgd', p.astype(v.dtype), v)


def swa_prompt(q, k, v, sinks):
    b, length = q.shape[:2]
    nc = length // CHUNK
    qc = q.reshape(b, nc, CHUNK, ATTN_KV_HEADS, ATTN_GROUP, HEAD_DIM)

    def band(t):
        tc = t.reshape(b, nc, CHUNK, ATTN_KV_HEADS, HEAD_DIM)
        tp = jnp.pad(tc, ((0, 0), (WINDOW_CHUNKS, 0), (0, 0), (0, 0), (0, 0)))
        return jnp.concatenate([tp[:, i:i + nc] for i in range(WINDOW_CHUNKS + 1)], axis=2)

    kb, vb = band(k), band(v)
    qi = jnp.arange(CHUNK)
    kj = jnp.arange((WINDOW_CHUNKS + 1) * CHUNK)
    dist = jnp.abs(qi[:, None] + WINDOW_CHUNKS * CHUNK - kj[None, :]).astype(jnp.float32)[None]
    kchunk = jnp.arange(nc)[:, None] - WINDOW_CHUNKS + kj[None, :] // CHUNK
    valid = (kchunk >= 0)[:, None, :]
    out = sink_attention(qc, kb, vb, dist, valid, sinks)
    return out.reshape(b, length, ATTN_Q_DIM)


def swa_sample(q, k, v, k_cache, v_cache, sinks):
    b, length = q.shape[:2]
    rows = k_cache.shape[1]
    kk = jnp.concatenate([k_cache.astype(k.dtype), k], axis=1)[:, None]
    vv = jnp.concatenate([v_cache.astype(v.dtype), v], axis=1)[:, None]
    qpos = PAST_LEN + jnp.arange(length)
    kpos = PAST_LEN - rows + jnp.arange(rows + length)
    qch, kch = qpos // CHUNK, kpos // CHUNK
    valid = (kch[None, :] <= qch[:, None]) & (kch[None, :] >= qch[:, None] - WINDOW_CHUNKS)
    dist = jnp.abs(qpos[:, None] - kpos[None, :]).astype(jnp.float32)
    qr = q.reshape(b, 1, length, ATTN_KV_HEADS, ATTN_GROUP, HEAD_DIM)
    out = sink_attention(qr, kk, vv, dist[None], valid[None], sinks)
    return out.reshape(b, length, ATTN_Q_DIM)


def ssd_scan(x, dt, a, bm, cm, h0):
    f32 = jnp.float32
    b, length = x.shape[:2]
    q = SSD_CHUNK if length % SSD_CHUNK == 0 else length
    nc = length // q
    xc = x.astype(f32).reshape(b, nc, q, SSD_GROUPS, SSD_HPG, SSD_HEAD_DIM)
    dtc = dt.astype(f32).reshape(b, nc, q, SSD_GROUPS, SSD_HPG)
    bc = bm.astype(f32).reshape(b, nc, q, SSD_GROUPS, SSD_STATE)
    cc = cm.astype(f32).reshape(b, nc, q, SSD_GROUPS, SSD_STATE)
    cum = jnp.cumsum(dtc * a.astype(f32).reshape(SSD_GROUPS, SSD_HPG), axis=2)
    xdt = xc * dtc[..., None]
    cum_t = jnp.moveaxis(cum, 2, -1)
    seg = cum_t[..., :, None] - cum_t[..., None, :]
    causal = jnp.tril(jnp.ones((q, q), dtype=bool))
    lmat = jnp.exp(jnp.where(causal, seg, -jnp.inf))
    cb = jnp.einsum('bcqgn,bcsgn->bcgqs', cc, bc)
    y_diag = jnp.einsum('bcgjqs,bcsgjp->bcqgjp', cb[:, :, :, None] * lmat, xdt)
    decay_end = jnp.exp(cum[:, :, -1:] - cum)
    chunk_state = jnp.einsum('bcsgn,bcsgjp->bcgjpn', bc, xdt * decay_end[..., None])
    chunk_decay = jnp.exp(cum[:, :, -1])

    def step(h, inp):
        dec, st = inp
        return dec[..., None, None] * h + st, h

    h_init = h0.astype(f32).reshape(b, SSD_GROUPS, SSD_HPG, SSD_HEAD_DIM, SSD_STATE)
    h_final, h_start = lax.scan(step, h_init,
                                (jnp.moveaxis(chunk_decay, 1, 0), jnp.moveaxis(chunk_state, 1, 0)))
    h_start = jnp.moveaxis(h_start, 0, 1)
    y_off = jnp.einsum('bcqgn,bcgjpn->bcqgjp', cc, h_start) * jnp.exp(cum)[..., None]
    y = (y_diag + y_off).reshape(b, length, SSD_HEADS, SSD_HEAD_DIM)
    return y, h_final.reshape(b, SSD_HEADS, SSD_HEAD_DIM, SSD_STATE)


def mixer_swa_ssd(h, w_in, w_out, sinks, conv_w, conv_b, dt_bias, a_log, d_skip, norm_g, cache):
    f32 = jnp.float32
    b, length, _ = h.shape
    i1 = ATTN_Q_DIM
    i2 = i1 + ATTN_KV_DIM
    i3 = i2 + ATTN_KV_DIM
    i4 = i3 + SSD_INNER
    i5 = i4 + SSD_CONV_DIM
    q, k, v, z, xbc, dt = jnp.split(h @ w_in, [i1, i2, i3, i4, i5], axis=-1)
    k = k.reshape(b, length, ATTN_KV_HEADS, HEAD_DIM)
    v = v.reshape(b, length, ATTN_KV_HEADS, HEAD_DIM)
    if cache is None:
        attn = swa_prompt(q, k, v, sinks)
        conv_prev = jnp.zeros((b, SSD_CONV - 1, SSD_CONV_DIM), h.dtype)
        h0 = jnp.zeros((b, SSD_HEADS, SSD_HEAD_DIM, SSD_STATE), f32)
        new_k, new_v = k[:, -WINDOW:], v[:, -WINDOW:]
    else:
        k_cache, v_cache, h0, conv_prev = cache
        attn = swa_sample(q, k, v, k_cache, v_cache, sinks)
        new_k, new_v = k, v
    xbc, new_conv = causal_dwconv(xbc, conv_prev, conv_w)
    xbc = jax.nn.silu(xbc + conv_b)
    xs, bm, cm = jnp.split(xbc, [SSD_INNER, SSD_INNER + SSD_GROUPS * SSD_STATE], axis=-1)
    xs = xs.reshape(b, length, SSD_HEADS, SSD_HEAD_DIM)
    dt = jax.nn.softplus((dt + dt_bias).astype(f32))
    a = -jnp.exp(a_log.astype(f32))
    y, h_new = ssd_scan(xs, dt, a, bm.reshape(b, length, SSD_GROUPS, SSD_STATE),
                        cm.reshape(b, length, SSD_GROUPS, SSD_STATE), h0)
    y = y + d_skip.astype(f32)[:, None] * xs.astype(f32)
    y = y.reshape(b, length, SSD_GROUPS, SSD_INNER // SSD_GROUPS) * \
        jax.nn.silu(z.astype(f32)).reshape(b, length, SSD_GROUPS, SSD_INNER // SSD_GROUPS)
    y = y * lax.rsqrt(jnp.mean(y * y, axis=-1, keepdims=True) + EPS)
    y = (y.reshape(b, length, SSD_INNER) * norm_g.astype(f32)).astype(h.dtype)
    out = jnp.concatenate([attn.astype(h.dtype), y], axis=-1) @ w_out
    return out, (new_k, new_v, h_new.astype(h.dtype), new_conv)


def mixer_sconv(h, w_in, conv_w, w_out, cache):
    b = h.shape[0]
    gate_b, gate_c, xi = jnp.split(h @ w_in, 3, axis=-1)
    prev = jnp.zeros((b, SCONV_WIDTH - 1, D_MODEL), h.dtype) if cache is None else cache
    u, new_buf = causal_dwconv(gate_c * xi, prev, conv_w)
    return (gate_b * u) @ w_out, new_buf


def setup_inputs(seed: int = 0) -> dict:
    key = jax.random.key(seed)
    ks = jax.random.split(key, 32)
    nrm = jax.random.normal
    f32 = jnp.float32
    swa_rows = min(WINDOW, PAST_LEN)
    dt0 = jnp.exp(jax.random.uniform(ks[18], (N_EVEN, SSD_HEADS), f32, np.log(1e-3), np.log(1e-1)))
    return {
        "x_prompt": nrm(ks[0], (BATCH, SEQ, D_MODEL), f32),
        "x_sample": nrm(ks[1], (DEC_BATCH, DEC_SEQ, D_MODEL), f32),
        "c_prompt": nrm(ks[2], (BATCH, D_MODEL), f32),
        "c_sample": nrm(ks[3], (DEC_BATCH, D_MODEL), f32),
        "cache_swa_k": nrm(ks[4], (N_EVEN, DEC_BATCH, swa_rows, ATTN_KV_HEADS, HEAD_DIM), f32),
        "cache_swa_v": nrm(ks[5], (N_EVEN, DEC_BATCH, swa_rows, ATTN_KV_HEADS, HEAD_DIM), f32),
        "state_ssd": 0.5 * nrm(ks[6], (N_EVEN, DEC_BATCH, SSD_HEADS, SSD_HEAD_DIM, SSD_STATE), f32),
        "state_ssd_conv": nrm(ks[7], (N_EVEN, DEC_BATCH, SSD_CONV - 1, SSD_CONV_DIM), f32),
        "state_sconv": nrm(ks[8], (N_ODD, DEC_BATCH, SCONV_WIDTH - 1, D_MODEL), f32),
        "norm_g": 1.0 + 0.02 * nrm(ks[9], (DEPTH, 3, D_MODEL), f32),
        "w_ada": 0.3 * D_MODEL ** -0.5 * nrm(ks[10], (DEPTH, D_MODEL, N_MOD * D_MODEL), f32),
        "b_ada": 0.02 * nrm(ks[11], (DEPTH, N_MOD * D_MODEL), f32),
        "w_ffn_gate": D_MODEL ** -0.5 * nrm(ks[12], (DEPTH, 2, D_MODEL, D_FF), f32),
        "w_ffn_up": D_MODEL ** -0.5 * nrm(ks[13], (DEPTH, 2, D_MODEL, D_FF), f32),
        "w_ffn_down": D_FF ** -0.5 * nrm(ks[14], (DEPTH, 2, D_FF, D_MODEL), f32),
        "w_in_mix0": D_MODEL ** -0.5 * nrm(ks[15], (N_EVEN, D_MODEL, IN0_DIM), f32),
        "w_out_mix0": MIX0_DIM ** -0.5 * nrm(ks[16], (N_EVEN, MIX0_DIM, D_MODEL), f32),
        "attn_sinks": 0.5 * nrm(ks[17], (N_EVEN, ATTN_HEADS), f32),
        "ssd_conv_w": SSD_CONV ** -0.5 * nrm(ks[19], (N_EVEN, SSD_CONV, SSD_CONV_DIM), f32),
        "ssd_conv_b": 0.02 * nrm(ks[20], (N_EVEN, SSD_CONV_DIM), f32),
        "ssd_dt_bias": dt0 + jnp.log(-jnp.expm1(-dt0)),
        "ssd_a_log": jnp.log(jax.random.uniform(ks[21], (N_EVEN, SSD_HEADS), f32, 1.0, 16.0)),
        "ssd_d": 1.0 + 0.02 * nrm(ks[22], (N_EVEN, SSD_HEADS), f32),
        "ssd_norm_g": 1.0 + 0.02 * nrm(ks[23], (N_EVEN, SSD_INNER), f32),
        "w_in_mix1": D_MODEL ** -0.5 * nrm(ks[24], (N_ODD, D_MODEL, 3 * D_MODEL), f32),
        "sconv_w": SCONV_WIDTH ** -0.5 * nrm(ks[25], (N_ODD, SCONV_WIDTH, D_MODEL), f32),
        "w_out_mix1": D_MODEL ** -0.5 * nrm(ks[26], (N_ODD, D_MODEL, D_MODEL), f32),
        "final_norm_g": 1.0 + 0.02 * nrm(ks[27], (D_MODEL,), f32),
    }


def reference(x_prompt, x_sample, c_prompt, c_sample, cache_swa_k, cache_swa_v, state_ssd,
              state_ssd_conv, state_sconv, norm_g, w_ada, b_ada, w_ffn_gate, w_ffn_up, w_ffn_down,
              w_in_mix0, w_out_mix0, attn_sinks, ssd_conv_w, ssd_conv_b, ssd_dt_bias, ssd_a_log,
              ssd_d, ssd_norm_g, w_in_mix1, sconv_w, w_out_mix1, final_norm_g):

    def run(x, c, with_cache):
        swa_k, swa_v, ssd_h, ssd_cv, sconv = [], [], [], [], []
        for l in range(DEPTH):
            mod = jax.nn.silu(c) @ w_ada[l] + b_ada[l]
            sh1, sc1, g1, sh2, sc2, g2, sh3, sc3, g3 = [m[:, None] for m in jnp.split(mod, N_MOD, axis=-1)]
            h = rmsnorm(x, norm_g[l, 0]) * (1 + sc1) + sh1
            x = x + 0.5 * g1 * swiglu(h, w_ffn_gate[l, 0], w_ffn_up[l, 0], w_ffn_down[l, 0])
            h = rmsnorm(x, norm_g[l, 1]) * (1 + sc2) + sh2
            i = l // 2
            if l % 2 == 0:
                cache = (cache_swa_k[i], cache_swa_v[i], state_ssd[i], state_ssd_conv[i]) if with_cache else None
                m, (nk, nv, nh, ncv) = mixer_swa_ssd(h, w_in_mix0[i], w_out_mix0[i], attn_sinks[i], ssd_conv_w[i],
                                                     ssd_conv_b[i], ssd_dt_bias[i], ssd_a_log[i], ssd_d[i],
                                                     ssd_norm_g[i], cache)
                swa_k.append(nk)
                swa_v.append(nv)
                ssd_h.append(nh)
                ssd_cv.append(ncv)
            else:
                cache = state_sconv[i] if with_cache else None
                m, nb = mixer_sconv(h, w_in_mix1[i], sconv_w[i], w_out_mix1[i], cache)
                sconv.append(nb)
            x = x + g2 * m
            h = rmsnorm(x, norm_g[l, 2]) * (1 + sc3) + sh3
            x = x + 0.5 * g3 * swiglu(h, w_ffn_gate[l, 1], w_ffn_up[l, 1], w_ffn_down[l, 1])
        return (rmsnorm(x, final_norm_g), jnp.stack(swa_k), jnp.stack(swa_v), jnp.stack(ssd_h),
                jnp.stack(ssd_cv), jnp.stack(sconv))

    y_prompt, k_p, v_p, h_p, cv_p, sc_p = run(x_prompt, c_prompt, False)
    y_sample, k_s, v_s, h_s, cv_s, sc_s = run(x_sample, c_sample, True)
    return (y_prompt, y_sample, k_p, v_p, h_p, cv_p, sc_p, k_s, v_s, h_s, cv_s, sc_s)
```

```python
import functools

import numpy as np
import jax
import jax.numpy as jnp
from jax import lax
from jax.experimental import pallas as pl
from jax.experimental.pallas import tpu as pltpu

F32 = jnp.float32
BF16 = jnp.bfloat16

EPS = 1e-6
CHUNK = 64
PAST_LEN = 2048
ATTN_HEADS = 16
ATTN_KV_HEADS = 2
ATTN_GROUP = ATTN_HEADS // ATTN_KV_HEADS
HEAD_DIM = 64
WINDOW = 128
WINDOW_CHUNKS = WINDOW // CHUNK
ATTN_Q_DIM = ATTN_HEADS * HEAD_DIM
ATTN_KV_DIM = ATTN_KV_HEADS * HEAD_DIM
SSD_HEADS = 16
SSD_HEAD_DIM = 64
SSD_INNER = SSD_HEADS * SSD_HEAD_DIM
SSD_STATE = 128
SSD_GROUPS = 2
SSD_HPG = SSD_HEADS // SSD_GROUPS
SSD_CONV = 4
SSD_BC_DIM = 2 * SSD_GROUPS * SSD_STATE
SSD_CONV_DIM = SSD_INNER + SSD_BC_DIM
SCONV_WIDTH = 3
N_MOD = 9

LANES = 128
SUBLANES = 8
VMEM_LIMIT = 56 * 1024 * 1024

P0_Q = 0
P0_Z = P0_Q + ATTN_Q_DIM
P0_X = P0_Z + SSD_INNER
P0_BC = P0_X + SSD_INNER
P0_K = P0_BC + SSD_BC_DIM
P0_V = P0_K + ATTN_KV_DIM
P0_DT = P0_V + ATTN_KV_DIM
P0_WIDTH = 4096
PROJ_TN = 512


def _params(*sem):
    return pltpu.CompilerParams(dimension_semantics=sem, vmem_limit_bytes=VMEM_LIMIT)


def _silu(x):
    return x / (1.0 + jnp.exp(-x))


def _softplus(x):
    return jnp.maximum(x, 0.0) + jnp.log(1.0 + jnp.exp(-jnp.abs(x)))


def _modnorm(x, g, sc, sh):
    y = x * lax.rsqrt(jnp.mean(x * x, axis=-1, keepdims=True) + EPS) * g
    return y * (1.0 + sc) + sh


def _mod_spec(mod_rows, d):
    if mod_rows == 1:
        return pl.BlockSpec((None, 1, d), lambda b, i, *_: (b, 0, 0))
    return pl.BlockSpec((None, mod_rows, d), lambda b, i, *_: (b, i, 0))


def _ada_kernel(c_ref, w_ref, b_ref, o_ref):
    c = c_ref[...]
    s = _silu(c).astype(BF16)
    o_ref[...] = jnp.dot(s, w_ref[...].astype(BF16), preferred_element_type=F32) + b_ref[...]


def ada_modulation(c, w_ada, b_ada, tn=1024):
    n_layers, d, n = w_ada.shape
    r = c.shape[0]
    return pl.pallas_call(
        _ada_kernel,
        grid=(n_layers, n // tn),
        in_specs=[
            pl.BlockSpec((r, d), lambda l, j: (0, 0)),
            pl.BlockSpec((None, d, tn), lambda l, j: (l, 0, j)),
            pl.BlockSpec((None, 1, tn), lambda l, j: (l, 0, j)),
        ],
        out_specs=pl.BlockSpec((None, r, tn), lambda l, j: (l, 0, j)),
        out_shape=jax.ShapeDtypeStruct((n_layers, r, n), F32),
        compiler_params=_params("arbitrary", "arbitrary"),
        name="ada_modulation",
    )(c, w_ada, b_ada.reshape(n_layers, 1, n))


def _ffn_kernel(x_ref, g_ref, sh_ref, sc_ref, gt_ref, wg_ref, wu_ref, wd_ref, *rest, n_ff, final):
    if final:
        fg_ref, o_ref, h_scr, acc = rest
    else:
        o_ref, h_scr, acc = rest
    j = pl.program_id(2)

    @pl.when(j == 0)
    def _():
        h = _modnorm(x_ref[...], g_ref[...], sc_ref[...], sh_ref[...])
        h_scr[...] = h.astype(BF16)
        acc[...] = jnp.zeros_like(acc)

    h = h_scr[...]
    g = jnp.dot(h, wg_ref[...], preferred_element_type=F32)
    u = jnp.dot(h, wu_ref[...], preferred_element_type=F32)
    a = (_silu(g) * u).astype(BF16)
    acc[...] += jnp.dot(a, wd_ref[...], preferred_element_type=F32)

    @pl.when(j == n_ff - 1)
    def _():
        xn = x_ref[...] + 0.5 * gt_ref[...] * acc[...]
        if final:
            xn = xn * lax.rsqrt(jnp.mean(xn * xn, axis=-1, keepdims=True) + EPS) * fg_ref[...]
        o_ref[...] = xn


def ffn_half(x, g, sh, sc, gt, wg, wu, wd, final_g=None, *, tm, tf=512):
    b, s, d = x.shape
    f = wg.shape[1]
    n_ff = f // tf
    mod_rows = 1 if sh.shape[1] == 1 else tm
    final = final_g is not None
    x_spec = pl.BlockSpec((None, tm, d), lambda b, i, j: (b, i, 0))
    in_specs = [
        x_spec,
        pl.BlockSpec((1, d), lambda b, i, j: (0, 0)),
        _mod_spec(mod_rows, d), _mod_spec(mod_rows, d), _mod_spec(mod_rows, d),
        pl.BlockSpec((d, tf), lambda b, i, j: (0, j)),
        pl.BlockSpec((d, tf), lambda b, i, j: (0, j)),
        pl.BlockSpec((tf, d), lambda b, i, j: (j, 0)),
    ]
    args = [x, g.reshape(1, d), sh, sc, gt, wg, wu, wd]
    if final:
        in_specs.append(pl.BlockSpec((1, d), lambda b, i, j: (0, 0)))
        args.append(final_g.reshape(1, d))
    return pl.pallas_call(
        functools.partial(_ffn_kernel, n_ff=n_ff, final=final),
        grid=(b, s // tm, n_ff),
        in_specs=in_specs,
        out_specs=x_spec,
        out_shape=jax.ShapeDtypeStruct(x.shape, F32),
        scratch_shapes=[pltpu.VMEM((tm, d), BF16), pltpu.VMEM((tm, d), F32)],
        compiler_params=_params("arbitrary", "arbitrary", "arbitrary"),
        name="ffn_half_final" if final else "ffn_half",
    )(*args)


def _proj_kernel(x_ref, g_ref, sh_ref, sc_ref, w_ref, o_ref, h_scr):
    @pl.when(pl.program_id(2) == 0)
    def _():
        h_scr[...] = _modnorm(x_ref[...], g_ref[...], sc_ref[...], sh_ref[...]).astype(BF16)

    o_ref[...] = jnp.dot(h_scr[...], w_ref[...], preferred_element_type=F32).astype(o_ref.dtype)


def modnorm_proj(x, g, sh, sc, w, *, tm, tn=PROJ_TN, out_dtype=F32):
    b, s, d = x.shape
    n = w.shape[1]
    mod_rows = 1 if sh.shape[1] == 1 else tm
    return pl.pallas_call(
        _proj_kernel,
        grid=(b, s // tm, n // tn),
        in_specs=[
            pl.BlockSpec((None, tm, d), lambda b, i, j: (b, i, 0)),
            pl.BlockSpec((1, d), lambda b, i, j: (0, 0)),
            _mod_spec(mod_rows, d), _mod_spec(mod_rows, d),
            pl.BlockSpec((d, tn), lambda b, i, j: (0, j)),
        ],
        out_specs=pl.BlockSpec((None, tm, tn), lambda b, i, j: (b, i, j)),
        out_shape=jax.ShapeDtypeStruct((b, s, n), out_dtype),
        scratch_shapes=[pltpu.VMEM((tm, d), BF16)],
        compiler_params=_params("arbitrary", "arbitrary", "arbitrary"),
        name="modnorm_proj",
    )(x, g.reshape(1, d), sh, sc, w)


def _attn_group(q, k, v, bias, sink):
    s = lax.dot_general(q, k, (((1,), (1,)), ((), ())), preferred_element_type=F32) + bias
    m = jnp.maximum(jnp.max(s, axis=1, keepdims=True), sink)
    p = jnp.exp(s - m)
    den = jnp.sum(p, axis=1, keepdims=True) + jnp.exp(sink - m)
    o = jnp.dot(p.astype(BF16), v, preferred_element_type=F32)
    return o / den


def _attn_rows(q_rows, kcat, vcat, bias_of, sink_ref):
    outs = []
    for h in range(ATTN_KV_HEADS):
        heads = [q_rows[:, (h * ATTN_GROUP + g) * HEAD_DIM:(h * ATTN_GROUP + g + 1) * HEAD_DIM]
                 for g in range(ATTN_GROUP)]
        qs = (jnp.concatenate(heads, axis=0) * (HEAD_DIM ** -0.5)).astype(BF16)
        o = _attn_group(qs, kcat[:, h * HEAD_DIM:(h + 1) * HEAD_DIM],
                        vcat[:, h * HEAD_DIM:(h + 1) * HEAD_DIM], bias_of(h), sink_ref[h])
        r = q_rows.shape[0]
        outs.extend(o[g * r:(g + 1) * r] for g in range(ATTN_GROUP))
    return jnp.concatenate(outs, axis=1)


def _attn_prompt_kernel(q_ref, kc_ref, kp_ref, vc_ref, vp_ref, bias_ref, sink_ref, o_ref, *, n_chunks):
    i = pl.program_id(1)
    kcat = jnp.concatenate([kp_ref[...], kc_ref[...]], axis=0).astype(BF16)
    vcat = jnp.concatenate([vp_ref[...], vc_ref[...]], axis=0).astype(BF16)
    span = (WINDOW_CHUNKS + 1) * CHUNK
    for cc in range(n_chunks):
        variant = jnp.minimum(i * n_chunks + cc, WINDOW_CHUNKS)
        rows = pl.ds(cc * CHUNK, CHUNK)
        o_ref[rows, :] = _attn_rows(
            q_ref[rows, :], kcat[cc * CHUNK:cc * CHUNK + span], vcat[cc * CHUNK:cc * CHUNK + span],
            lambda h: bias_ref[variant, h], sink_ref).astype(o_ref.dtype)


def _alibi_slopes():
    return 2.0 ** (-8.0 * np.arange(1, ATTN_HEADS + 1) / ATTN_HEADS)


def _stack_bias(dist, valid):
    slopes = _alibi_slopes().reshape(ATTN_KV_HEADS, ATTN_GROUP)
    bias = -slopes[:, :, None, None] * dist[None, None]
    bias = np.where(valid[None, None], bias, -np.inf)
    q, s = dist.shape
    return bias.reshape(ATTN_KV_HEADS, ATTN_GROUP * q, s).astype(np.float32)


def _stack_sinks(sinks, rows):
    s = sinks.astype(F32).reshape(ATTN_KV_HEADS, ATTN_GROUP, 1, 1)
    return jnp.broadcast_to(s, (ATTN_KV_HEADS, ATTN_GROUP, rows, 1)).reshape(ATTN_KV_HEADS, ATTN_GROUP * rows, 1)


def attn_prompt(p0, sinks, *, qb=256, out_dtype=F32):
    b, s, _ = p0.shape
    n_chunks = qb // CHUNK
    span = (WINDOW_CHUNKS + 1) * CHUNK
    qi = np.arange(CHUNK)
    kj = np.arange(span)
    dist = np.abs(qi[:, None] + WINDOW_CHUNKS * CHUNK - kj[None, :]).astype(np.float64)
    bias = np.stack([
        _stack_bias(dist, np.broadcast_to(c - WINDOW_CHUNKS + kj[None, :] // CHUNK >= 0, dist.shape))
        for c in range(WINDOW_CHUNKS + 1)])
    kv_w = ATTN_KV_DIM
    cur = lambda col: pl.BlockSpec((None, qb, kv_w), lambda b, i: (b, i, col))
    prev = lambda col: pl.BlockSpec(
        (None, WINDOW, kv_w), lambda b, i: (b, jnp.maximum(i * (qb // WINDOW) - 1, 0), col))
    return pl.pallas_call(
        functools.partial(_attn_prompt_kernel, n_chunks=n_chunks),
        grid=(b, s // qb),
        in_specs=[
            pl.BlockSpec((None, qb, ATTN_Q_DIM), lambda b, i: (b, i, P0_Q // ATTN_Q_DIM)),
            cur(P0_K // kv_w), prev(P0_K // kv_w), cur(P0_V // kv_w), prev(P0_V // kv_w),
            pl.BlockSpec(bias.shape, lambda b, i: (0, 0, 0, 0)),
            pl.BlockSpec((ATTN_KV_HEADS, ATTN_GROUP * CHUNK, 1), lambda b, i: (0, 0, 0)),
        ],
        out_specs=pl.BlockSpec((None, qb, ATTN_Q_DIM), lambda b, i: (b, i, 0)),
        out_shape=jax.ShapeDtypeStruct((b, s, ATTN_Q_DIM), out_dtype),
        compiler_params=_params("arbitrary", "arbitrary"),
        name="attn_prompt",
    )(p0, p0, p0, p0, p0, jnp.asarray(bias), _stack_sinks(sinks, CHUNK))


def _attn_sample_kernel(q_ref, kn_ref, vn_ref, kc_ref, vc_ref, bias_ref, sink_ref, o_ref):
    kcat = jnp.concatenate([kc_ref[...], kn_ref[...]], axis=0).astype(BF16)
    vcat = jnp.concatenate([vc_ref[...], vn_ref[...]], axis=0).astype(BF16)
    o_ref[...] = _attn_rows(q_ref[...], kcat, vcat, lambda h: bias_ref[h], sink_ref).astype(o_ref.dtype)


def attn_sample(p0, k_cache, v_cache, sinks, *, out_dtype=F32):
    b, length, _ = p0.shape
    rows = k_cache.shape[1]
    qpos = PAST_LEN + np.arange(length)
    kpos = PAST_LEN - rows + np.arange(rows + length)
    qch, kch = qpos // CHUNK, kpos // CHUNK
    valid = (kch[None, :] <= qch[:, None]) & (kch[None, :] >= qch[:, None] - WINDOW_CHUNKS)
    dist = np.abs(qpos[:, None] - kpos[None, :]).astype(np.float64)
    bias = _stack_bias(dist, valid)
    kv_w = ATTN_KV_DIM
    new = lambda col: pl.BlockSpec((None, length, kv_w), lambda b: (b, 0, col))
    cache = pl.BlockSpec((None, rows, kv_w), lambda b: (b, 0, 0))
    return pl.pallas_call(
        _attn_sample_kernel,
        grid=(b,),
        in_specs=[
            pl.BlockSpec((None, length, ATTN_Q_DIM), lambda b: (b, 0, P0_Q // ATTN_Q_DIM)),
            new(P0_K // kv_w), new(P0_V // kv_w), cache, cache,
            pl.BlockSpec(bias.shape, lambda b: (0, 0, 0)),
            pl.BlockSpec((ATTN_KV_HEADS, ATTN_GROUP * length, 1), lambda b: (0, 0, 0)),
        ],
        out_specs=pl.BlockSpec((None, length, ATTN_Q_DIM), lambda b: (b, 0, 0)),
        out_shape=jax.ShapeDtypeStruct((b, length, ATTN_Q_DIM), out_dtype),
        compiler_params=_params("arbitrary"),
        name="attn_sample",
    )(p0, p0, p0, k_cache, v_cache, jnp.asarray(bias), _stack_sinks(sinks, length))


def _causal_conv(pad_ref, prev_ref, cur, w_ref, b_ref, first, q):
    k = w_ref.shape[0]

    @pl.when(first)
    def _():
        pad_ref[pl.ds(0, SUBLANES), :] = prev_ref[...]

    pad_ref[pl.ds(SUBLANES, q), :] = cur
    acc = b_ref[...] + w_ref[k - 1:k, :] * cur
    for t in range(k - 1):
        acc = acc + w_ref[t:t + 1, :] * pad_ref[pl.ds(SUBLANES - (k - 1) + t, q), :]
    pad_ref[pl.ds(0, SUBLANES), :] = pad_ref[pl.ds(q, SUBLANES), :]
    return acc


def _split3(x):
    hi = x.astype(BF16)
    r = x - hi.astype(F32)
    mid = r.astype(BF16)
    lo = (r - mid.astype(F32)).astype(BF16)
    return hi, mid, lo


def _ssd_kernel(x_ref, bc_ref, z_ref, dt_ref, xprev_ref, bcprev_ref, h0_ref,
                wx_ref, bx_ref, wbc_ref, bbc_ref, dtb_ref, alog_ref, dskip_ref, ng_ref,
                y_ref, hout_ref, xpad, bcpad, h_scr, *, q, n_blocks, valid_len):
    i = pl.program_id(1)
    first = i == 0

    @pl.when(first)
    def _():
        h_scr[...] = h0_ref[...]

    xs = _silu(_causal_conv(xpad, xprev_ref, x_ref[...], wx_ref, bx_ref, first, q))
    bc = _silu(_causal_conv(bcpad, bcprev_ref, bc_ref[...], wbc_ref, bbc_ref, first, q))
    xs_b = xs.astype(BF16)

    dt = _softplus(dt_ref[...] + dtb_ref[...])
    if valid_len < q:
        row = lax.broadcasted_iota(jnp.int32, dt.shape, 0)
        dt = jnp.where(row < valid_len, dt, 0.0)
    da = dt * (-jnp.exp(alog_ref[...]))
    r_io = lax.broadcasted_iota(jnp.int32, (q, q), 0)
    c_io = lax.broadcasted_iota(jnp.int32, (q, q), 1)
    causal = c_io <= r_io
    tri = jnp.where(causal, 1.0, 0.0).astype(BF16)
    cum = sum(jnp.dot(tri, part, preferred_element_type=F32) for part in _split3(da))
    cum_t = cum.T
    dt_t = dt.T
    lane = lax.broadcasted_iota(jnp.int32, (1, LANES), 1)
    low_half = lane < SSD_HEAD_DIM

    y_pairs = []
    for g in range(SSD_GROUPS):
        b_g = bc[:, g * SSD_STATE:(g + 1) * SSD_STATE]
        c_g = bc[:, (SSD_GROUPS + g) * SSD_STATE:(SSD_GROUPS + g + 1) * SSD_STATE]
        cb = lax.dot_general(c_g.astype(BF16), b_g.astype(BF16), (((1,), (1,)), ((), ())),
                             preferred_element_type=F32)
        b_t = b_g.T
        for pr in range(g * SSD_HPG // 2, (g + 1) * SSD_HPG // 2):
            x_pair = xs_b[:, pr * LANES:(pr + 1) * LANES]
            h_pair = h_scr[pr]
            h_pair_b = h_pair.astype(BF16)
            ys, states, decays = [], [], []
            for j in (2 * pr, 2 * pr + 1):
                col = jnp.broadcast_to(cum[:, j:j + 1], (q, q))
                row = cum_t[j:j + 1, :]
                decay_in = jnp.exp(jnp.where(causal, col - row, -jnp.inf))
                m = (cb * decay_in * dt_t[j:j + 1, :]).astype(BF16)
                c_scaled = (c_g * jnp.exp(col[:, :SSD_STATE])).astype(BF16)
                ys.append(jnp.dot(m, x_pair, preferred_element_type=F32)
                          + jnp.dot(c_scaled, h_pair_b, preferred_element_type=F32))
                last = row[:, q - 1:q]
                w_row = dt_t[j:j + 1, :] * jnp.exp(last - row)
                states.append(jnp.dot((b_t * w_row).astype(BF16), x_pair, preferred_element_type=F32))
                decays.append(jnp.exp(last))
            h_scr[pr] = (jnp.where(low_half, decays[0], decays[1]) * h_pair
                         + jnp.where(low_half, states[0], states[1]))
            y_pairs.append(jnp.where(low_half, ys[0], ys[1]))
    y = jnp.concatenate(y_pairs, axis=1) + dskip_ref[...] * xs
    y = y * _silu(z_ref[...])
    half = SSD_INNER // SSD_GROUPS
    normed = []
    for g in range(SSD_GROUPS):
        yg = y[:, g * half:(g + 1) * half]
        normed.append(yg * lax.rsqrt(jnp.mean(yg * yg, axis=-1, keepdims=True) + EPS))
    y_ref[...] = (jnp.concatenate(normed, axis=1) * ng_ref[...]).astype(y_ref.dtype)

    @pl.when(i == n_blocks - 1)
    def _():
        hout_ref[...] = h_scr[...]


def ssd_mixer(p0, x_prev, bc_prev, h0, conv_w, conv_b, dt_bias, a_log, d_skip, norm_g, *, q, valid_len=None,
              out_dtype=F32):
    b, s, _ = p0.shape
    n_blocks = s // q
    valid_len = q if valid_len is None else valid_len
    n_pairs = SSD_HEADS // 2
    pad128 = lambda v: jnp.pad(v.astype(F32), (0, LANES - v.shape[0])).reshape(1, LANES)
    col = lambda width, off: pl.BlockSpec((None, q, width), lambda b, i: (b, i, off // width))
    const = lambda shape: pl.BlockSpec(shape, lambda b, i: (0,) * len(shape))
    per_b = lambda shape: pl.BlockSpec((None,) + shape, lambda b, i: (b,) + (0,) * len(shape))
    h_shape = (n_pairs, SSD_STATE, LANES)
    return pl.pallas_call(
        functools.partial(_ssd_kernel, q=q, n_blocks=n_blocks, valid_len=valid_len),
        grid=(b, n_blocks),
        in_specs=[
            col(SSD_INNER, P0_X), col(SSD_BC_DIM, P0_BC), col(SSD_INNER, P0_Z), col(LANES, P0_DT),
            per_b((SUBLANES, SSD_INNER)), per_b((SUBLANES, SSD_BC_DIM)), per_b(h_shape),
            const((SSD_CONV, SSD_INNER)), const((1, SSD_INNER)),
            const((SSD_CONV, SSD_BC_DIM)), const((1, SSD_BC_DIM)),
            const((1, LANES)), const((1, LANES)), const((1, SSD_INNER)), const((1, SSD_INNER)),
        ],
        out_specs=[
            pl.BlockSpec((None, q, SSD_INNER), lambda b, i: (b, i, 0)),
            per_b(h_shape),
        ],
        out_shape=[
            jax.ShapeDtypeStruct((b, s, SSD_INNER), out_dtype),
            jax.ShapeDtypeStruct((b,) + h_shape, F32),
        ],
        scratch_shapes=[
            pltpu.VMEM((q + SUBLANES, SSD_INNER), F32),
            pltpu.VMEM((q + SUBLANES, SSD_BC_DIM), F32),
            pltpu.VMEM(h_shape, F32),
        ],
        compiler_params=_params("arbitrary", "arbitrary"),
        name="ssd_mixer",
    )(p0, p0, p0, p0, x_prev, bc_prev, h0,
      conv_w[:, :SSD_INNER], conv_b[:SSD_INNER].reshape(1, -1),
      conv_w[:, SSD_INNER:], conv_b[SSD_INNER:].reshape(1, -1),
      pad128(dt_bias), pad128(a_log),
      jnp.repeat(d_skip.astype(F32), SSD_HEAD_DIM).reshape(1, -1), norm_g.astype(F32).reshape(1, -1))


def _pair_state(h):
    b = h.shape[0]
    h = h.astype(F32).reshape(b, SSD_HEADS // 2, 2, SSD_HEAD_DIM, SSD_STATE)
    return h.transpose(0, 1, 4, 2, 3).reshape(b, SSD_HEADS // 2, SSD_STATE, 2 * SSD_HEAD_DIM)


def _unpair_state(h):
    b = h.shape[0]
    h = h.reshape(b, SSD_HEADS // 2, SSD_STATE, 2, SSD_HEAD_DIM)
    return h.transpose(0, 1, 3, 4, 2).reshape(b, SSD_HEADS, SSD_HEAD_DIM, SSD_STATE)


def _tail_rows(prev, width):
    b, k1, _ = prev.shape
    return jnp.pad(prev.astype(F32), ((0, 0), (SUBLANES - k1, 0), (0, 0)))


def _outproj_kernel(x_ref, a_ref, y_ref, gt_ref, wa_ref, wy_ref, o_ref):
    m = (jnp.dot(a_ref[...].astype(BF16), wa_ref[...], preferred_element_type=F32)
         + jnp.dot(y_ref[...].astype(BF16), wy_ref[...], preferred_element_type=F32))
    o_ref[...] = x_ref[...] + gt_ref[...] * m


def mix0_out(x, attn, y, gt, w_out, *, tm):
    b, s, d = x.shape
    da, dy = attn.shape[2], y.shape[2]
    mod_rows = 1 if gt.shape[1] == 1 else tm
    x_spec = pl.BlockSpec((None, tm, d), lambda b, i: (b, i, 0))
    return pl.pallas_call(
        _outproj_kernel,
        grid=(b, s // tm),
        in_specs=[
            x_spec,
            pl.BlockSpec((None, tm, da), lambda b, i: (b, i, 0)),
            pl.BlockSpec((None, tm, dy), lambda b, i: (b, i, 0)),
            _mod_spec(mod_rows, d),
            pl.BlockSpec((da, d), lambda b, i: (0, 0)),
            pl.BlockSpec((dy, d), lambda b, i: (0, 0)),
        ],
        out_specs=x_spec,
        out_shape=jax.ShapeDtypeStruct(x.shape, F32),
        compiler_params=_params("arbitrary", "arbitrary"),
        name="mix0_out",
    )(x, attn, y, gt, w_out[:da], w_out[da:])


def _sconv_kernel(x_ref, gb_ref, gc_ref, xi_ref, prev_ref, gt_ref, cw_ref, w_ref, o_ref, buf_ref, vpad, *, tm):
    first = pl.program_id(1) == 0
    v = gc_ref[...].astype(F32) * xi_ref[...].astype(F32)
    k = cw_ref.shape[0]

    @pl.when(first)
    def _():
        vpad[pl.ds(0, SUBLANES), :] = prev_ref[...]

    vpad[pl.ds(SUBLANES, tm), :] = v
    u = cw_ref[k - 1:k, :] * v
    for t in range(k - 1):
        u = u + cw_ref[t:t + 1, :] * vpad[pl.ds(SUBLANES - (k - 1) + t, tm), :]
    tail = vpad[pl.ds(tm, SUBLANES), :]
    vpad[pl.ds(0, SUBLANES), :] = tail
    buf_ref[...] = tail
    r = (gb_ref[...].astype(F32) * u).astype(BF16)
    o_ref[...] = x_ref[...] + gt_ref[...] * jnp.dot(r, w_ref[...], preferred_element_type=F32)


def sconv_mixer(x, p1, prev, gt, conv_w, w_out, *, tm):
    b, s, d = x.shape
    x_spec = pl.BlockSpec((None, tm, d), lambda b, i: (b, i, 0))
    part = lambda c: pl.BlockSpec((None, tm, d), lambda b, i: (b, i, c))
    buf_spec = pl.BlockSpec((None, SUBLANES, d), lambda b, i: (b, 0, 0))
    return pl.pallas_call(
        functools.partial(_sconv_kernel, tm=tm),
        grid=(b, s // tm),
        in_specs=[
            x_spec, part(0), part(1), part(2), buf_spec, _mod_spec(1, d),
            pl.BlockSpec((SCONV_WIDTH, d), lambda b, i: (0, 0)),
            pl.BlockSpec((d, d), lambda b, i: (0, 0)),
        ],
        out_specs=[x_spec, buf_spec],
        out_shape=[jax.ShapeDtypeStruct(x.shape, F32), jax.ShapeDtypeStruct((b, SUBLANES, d), F32)],
        scratch_shapes=[pltpu.VMEM((tm + SUBLANES, d), F32)],
        compiler_params=_params("arbitrary", "arbitrary"),
        name="sconv_mixer",
    )(x, p1, p1, p1, prev, gt, conv_w.astype(F32), w_out)


def _arrange_w_in0(w):
    i1 = ATTN_Q_DIM
    i2 = i1 + ATTN_KV_DIM
    i3 = i2 + ATTN_KV_DIM
    i4 = i3 + SSD_INNER
    i5 = i4 + SSD_CONV_DIM
    q, k, v, z, xbc, dt = (w[:, a:b] for a, b in ((0, i1), (i1, i2), (i2, i3), (i3, i4), (i4, i5), (i5, w.shape[1])))
    pad = jnp.zeros((w.shape[0], P0_WIDTH - P0_DT - dt.shape[1]), w.dtype)
    return jnp.concatenate([q, z, xbc, k, v, dt, pad], axis=1).astype(BF16)


def _split_mod(mod):
    return [m[:, None, :] for m in jnp.split(mod, N_MOD, axis=-1)]


def _per_token(m, length):
    b, _, d = m.shape
    return jnp.broadcast_to(m, (b, length, d)).reshape(1, b * length, d)


def _trunk(x, mods, weights, cache, *, tm, tm_mix, q_ssd, qb_attn):
    b, s, d = x.shape
    flat = cache is not None
    if flat:
        as_rows = lambda t: t.reshape(1, b * s, t.shape[-1])
        mod_of = lambda m: _per_token(m, s)
    else:
        as_rows = lambda t: t
        mod_of = lambda m: m
    unrow = lambda t: t.reshape(b, s, t.shape[-1])
    w = weights

    def ffn(xr, l, k, sh, sc, gt, final_g=None):
        return ffn_half(xr, w["norm_g"][l, 2 * k], mod_of(sh), mod_of(sc), mod_of(gt),
                        w["ffn_gate"][l][k], w["ffn_up"][l][k], w["ffn_down"][l][k], final_g, tm=tm)

    sh1, sc1, g1, sh2, sc2, g2, sh3, sc3, g3 = mods[0]
    xr = ffn(as_rows(x), 0, 0, sh1, sc1, g1)
    p0 = unrow(modnorm_proj(xr, w["norm_g"][0, 1], mod_of(sh2), mod_of(sc2), w["w_in0"], tm=tm))
    if cache is None:
        attn = attn_prompt(p0, w["sinks"], qb=qb_attn)
        x_prev = jnp.zeros((b, SUBLANES, SSD_INNER), F32)
        bc_prev = jnp.zeros((b, SUBLANES, SSD_BC_DIM), F32)
        h0 = jnp.zeros((b, SSD_HEADS // 2, SSD_STATE, LANES), F32)
        y, h_new = ssd_mixer(p0, x_prev, bc_prev, h0, *w["ssd"], q=q_ssd)
        new_k = p0[:, s - WINDOW:, P0_K:P0_K + ATTN_KV_DIM]
        new_v = p0[:, s - WINDOW:, P0_V:P0_V + ATTN_KV_DIM]
        sconv_prev = jnp.zeros((b, SUBLANES, d), F32)
    else:
        k_cache, v_cache, h_state, conv_prev, sconv_state = cache
        rows = k_cache.shape[1]
        attn = attn_sample(p0, k_cache.reshape(b, rows, ATTN_KV_DIM), v_cache.reshape(b, rows, ATTN_KV_DIM),
                           w["sinks"])
        tail = _tail_rows(conv_prev, SSD_CONV_DIM)
        p0_pad = jnp.pad(p0, ((0, 0), (0, q_ssd - s), (0, 0)))
        y, h_new = ssd_mixer(p0_pad, tail[:, :, :SSD_INNER], tail[:, :, SSD_INNER:], _pair_state(h_state),
                             *w["ssd"], q=q_ssd, valid_len=s)
        y = y[:, :s]
        new_k = p0[:, :, P0_K:P0_K + ATTN_KV_DIM]
        new_v = p0[:, :, P0_V:P0_V + ATTN_KV_DIM]
        sconv_prev = _tail_rows(sconv_state, d)
    xbc_raw = p0[:, s - (SSD_CONV - 1):, P0_X:P0_X + SSD_CONV_DIM]
    kv_shape = (b, new_k.shape[1], ATTN_KV_HEADS, HEAD_DIM)
    states0 = (new_k.reshape(kv_shape), new_v.reshape(kv_shape), _unpair_state(h_new), xbc_raw)
    xr = mix0_out(xr, as_rows(attn), as_rows(y), mod_of(g2), w["w_out0"], tm=tm_mix)
    xr = ffn(xr, 0, 1, sh3, sc3, g3)

    sh1, sc1, g1, sh2, sc2, g2, sh3, sc3, g3 = mods[1]
    xr = ffn(xr, 1, 0, sh1, sc1, g1)
    p1 = unrow(modnorm_proj(xr, w["norm_g"][1, 1], mod_of(sh2), mod_of(sc2), w["w_in1"], tm=tm))
    xm, buf = sconv_mixer(unrow(xr), p1, sconv_prev, g2, w["sconv_w"], w["w_out1"], tm=min(tm_mix, s))
    sconv_new = buf[:, SUBLANES - (SCONV_WIDTH - 1):]
    y_out = unrow(ffn(as_rows(xm), 1, 1, sh3, sc3, g3, final_g=w["final_g"]))
    return y_out, states0, sconv_new


def kernel(x_prompt, x_sample, c_prompt, c_sample, cache_swa_k, cache_swa_v, state_ssd, state_ssd_conv, state_sconv, norm_g, w_ada, b_ada, w_ffn_gate, w_ffn_up, w_ffn_down, w_in_mix0, w_out_mix0, attn_sinks, ssd_conv_w, ssd_conv_b, ssd_dt_bias, ssd_a_log, ssd_d, ssd_norm_g, w_in_mix1, sconv_w, w_out_mix1, final_norm_g):
    bp, bs = c_prompt.shape[0], c_sample.shape[0]
    c_all = jnp.concatenate([c_prompt, c_sample], axis=0)
    c_rows = -(-c_all.shape[0] // SUBLANES) * SUBLANES
    c_all = jnp.pad(c_all, ((0, c_rows - c_all.shape[0]), (0, 0)))
    mod = ada_modulation(c_all, w_ada, b_ada)
    mods_p = [_split_mod(mod[l, :bp]) for l in range(mod.shape[0])]
    mods_s = [_split_mod(mod[l, bp:bp + bs]) for l in range(mod.shape[0])]

    weights = {
        "norm_g": norm_g,
        "ffn_gate": w_ffn_gate.astype(BF16), "ffn_up": w_ffn_up.astype(BF16), "ffn_down": w_ffn_down.astype(BF16),
        "w_in0": _arrange_w_in0(w_in_mix0[0]), "w_out0": w_out_mix0[0].astype(BF16),
        "sinks": attn_sinks[0],
        "ssd": (ssd_conv_w[0].astype(F32), ssd_conv_b[0].astype(F32), ssd_dt_bias[0], ssd_a_log[0], ssd_d[0],
                ssd_norm_g[0]),
        "w_in1": w_in_mix1[0].astype(BF16), "sconv_w": sconv_w[0], "w_out1": w_out_mix1[0].astype(BF16),
        "final_g": final_norm_g,
    }
    seq = x_prompt.shape[1]
    y_p, (k_p, v_p, h_p, cv_p), sc_p = _trunk(
        x_prompt, mods_p, weights, None, tm=min(512, seq), tm_mix=min(256, seq), q_ssd=min(256, seq),
        qb_attn=min(256, seq))
    n_s = x_sample.shape[0] * x_sample.shape[1]
    cache = (cache_swa_k[0], cache_swa_v[0], state_ssd[0], state_ssd_conv[0], state_sconv[0])
    y_s, (k_s, v_s, h_s, cv_s), sc_s = _trunk(
        x_sample, mods_s, weights, cache, tm=n_s, tm_mix=n_s, q_ssd=LANES, qb_attn=None)
    stack = lambda t: t[None]
    return (y_p, y_s, stack(k_p), stack(v_p), stack(h_p), stack(cv_p), stack(sc_p),
            stack(k_s), stack(v_s), stack(h_s), stack(cv_s), stack(sc_s))
```

```python
import functools

import numpy as np
import jax
import jax.numpy as jnp
from jax import lax
from jax.experimental import pallas as pl
from jax.experimental.pallas import tpu as pltpu

F32 = jnp.float32
BF16 = jnp.bfloat16

EPS = 1e-6
CHUNK = 64
PAST_LEN = 2048
ATTN_HEADS = 16
ATTN_KV_HEADS = 2
ATTN_GROUP = ATTN_HEADS // ATTN_KV_HEADS
HEAD_DIM = 64
WINDOW = 128
WINDOW_CHUNKS = WINDOW // CHUNK
ATTN_Q_DIM = ATTN_HEADS * HEAD_DIM
ATTN_KV_DIM = ATTN_KV_HEADS * HEAD_DIM
SSD_HEADS = 16
SSD_HEAD_DIM = 64
SSD_INNER = SSD_HEADS * SSD_HEAD_DIM
SSD_STATE = 128
SSD_GROUPS = 2
SSD_HPG = SSD_HEADS // SSD_GROUPS
SSD_CONV = 4
SSD_BC_DIM = 2 * SSD_GROUPS * SSD_STATE
SSD_CONV_DIM = SSD_INNER + SSD_BC_DIM
SCONV_WIDTH = 3
N_MOD = 9

LANES = 128
SUBLANES = 8
MXU_DIM = 256
VMEM_LIMIT = 56 * 1024 * 1024

PROJ_TN = 512
PA_Q = 0
PA_Z = PA_Q + ATTN_Q_DIM
PA_X = PA_Z + SSD_INNER
PA_BC = PA_X + SSD_INNER
PA_WIDTH = PA_BC + SSD_BC_DIM
PB_K = 0
PB_V = PB_K + ATTN_KV_DIM
PB_DT = PB_V + ATTN_KV_DIM
PB_WIDTH = PROJ_TN
ATTN_KEY_PAD = MXU_DIM
assert HEAD_DIM * 2 == LANES and ATTN_KV_DIM == LANES and PA_WIDTH % PROJ_TN == 0


def _params(*sem):
    return pltpu.CompilerParams(dimension_semantics=sem, vmem_limit_bytes=VMEM_LIMIT)


def _silu(x):
    return x / (1.0 + jnp.exp(-x))


def _softplus(x):
    return jnp.maximum(x, 0.0) + jnp.log(1.0 + jnp.exp(-jnp.abs(x)))


def _modnorm(x, g, sc, sh):
    y = x * lax.rsqrt(jnp.mean(x * x, axis=-1, keepdims=True) + EPS) * g
    return y * (1.0 + sc) + sh


def _mod_spec(mod_rows, d):
    if mod_rows == 1:
        return pl.BlockSpec((None, 1, d), lambda b, i, *_: (b, 0, 0))
    return pl.BlockSpec((None, mod_rows, d), lambda b, i, *_: (b, i, 0))


def _ada_kernel(c_ref, w_ref, b_ref, o_ref):
    c = c_ref[...]
    s = _silu(c).astype(BF16)
    o_ref[...] = jnp.dot(s, w_ref[...].astype(BF16), preferred_element_type=F32) + b_ref[...]


def ada_modulation(c, w_ada, b_ada, tn=1024):
    n_layers, d, n = w_ada.shape
    r = c.shape[0]
    return pl.pallas_call(
        _ada_kernel,
        grid=(n_layers, n // tn),
        in_specs=[
            pl.BlockSpec((r, d), lambda l, j: (0, 0)),
            pl.BlockSpec((None, d, tn), lambda l, j: (l, 0, j)),
            pl.BlockSpec((None, 1, tn), lambda l, j: (l, 0, j)),
        ],
        out_specs=pl.BlockSpec((None, r, tn), lambda l, j: (l, 0, j)),
        out_shape=jax.ShapeDtypeStruct((n_layers, r, n), F32),
        compiler_params=_params("arbitrary", "arbitrary"),
        name="ada_modulation",
    )(c, w_ada, b_ada.reshape(n_layers, 1, n))


def _ffn_kernel(x_ref, g_ref, sh_ref, sc_ref, gt_ref, wg_ref, wu_ref, wd_ref, *rest, n_ff, final):
    if final:
        fg_ref, o_ref, h_scr, acc = rest
    else:
        o_ref, h_scr, acc = rest
    j = pl.program_id(2)

    @pl.when(j == 0)
    def _():
        h = _modnorm(x_ref[...], g_ref[...], sc_ref[...], sh_ref[...])
        h_scr[...] = h.astype(BF16)
        acc[...] = jnp.zeros_like(acc)

    h = h_scr[...]
    g = jnp.dot(h, wg_ref[...], preferred_element_type=F32)
    u = jnp.dot(h, wu_ref[...], preferred_element_type=F32)
    a = (_silu(g) * u).astype(BF16)
    acc[...] += jnp.dot(a, wd_ref[...], preferred_element_type=F32)

    @pl.when(j == n_ff - 1)
    def _():
        xn = x_ref[...] + 0.5 * gt_ref[...] * acc[...]
        if final:
            xn = xn * lax.rsqrt(jnp.mean(xn * xn, axis=-1, keepdims=True) + EPS) * fg_ref[...]
        o_ref[...] = xn


def ffn_half(x, g, sh, sc, gt, wg, wu, wd, layer, half, final_g=None, *, tm, tf=512):
    b, s, d = x.shape
    f = wg.shape[-1]
    n_ff = f // tf
    mod_rows = 1 if sh.shape[1] == 1 else tm
    final = final_g is not None
    x_spec = pl.BlockSpec((None, tm, d), lambda b, i, j: (b, i, 0))
    in_specs = [
        x_spec,
        pl.BlockSpec((1, d), lambda b, i, j: (0, 0)),
        _mod_spec(mod_rows, d), _mod_spec(mod_rows, d), _mod_spec(mod_rows, d),
        pl.BlockSpec((None, None, d, tf), lambda b, i, j: (layer, half, 0, j)),
        pl.BlockSpec((None, None, d, tf), lambda b, i, j: (layer, half, 0, j)),
        pl.BlockSpec((None, None, tf, d), lambda b, i, j: (layer, half, j, 0)),
    ]
    args = [x, g.reshape(1, d), sh, sc, gt, wg, wu, wd]
    if final:
        in_specs.append(pl.BlockSpec((1, d), lambda b, i, j: (0, 0)))
        args.append(final_g.reshape(1, d))
    return pl.pallas_call(
        functools.partial(_ffn_kernel, n_ff=n_ff, final=final),
        grid=(b, s // tm, n_ff),
        in_specs=in_specs,
        out_specs=x_spec,
        out_shape=jax.ShapeDtypeStruct(x.shape, F32),
        scratch_shapes=[pltpu.VMEM((tm, d), BF16), pltpu.VMEM((tm, d), F32)],
        compiler_params=_params("arbitrary", "arbitrary", "arbitrary"),
        name="ffn_half_final" if final else "ffn_half",
    )(*args)


def _proj_kernel(x_ref, g_ref, sh_ref, sc_ref, w_ref, *rest, n_main):
    j = pl.program_id(2)
    h_scr = rest[-1]

    @pl.when(j == 0)
    def _():
        h_scr[...] = _modnorm(x_ref[...], g_ref[...], sc_ref[...], sh_ref[...]).astype(BF16)

    r = jnp.dot(h_scr[...], w_ref[...], preferred_element_type=F32)
    if n_main is None:
        rest[0][...] = r.astype(rest[0].dtype)
    else:
        main_ref, tail_ref = rest[0], rest[1]

        @pl.when(j < n_main)
        def _():
            main_ref[...] = r.astype(main_ref.dtype)

        @pl.when(j >= n_main)
        def _():
            tail_ref[...] = r


def modnorm_proj(x, g, sh, sc, w, *, tm, tn=PROJ_TN, tail_width=None):
    b, s, d = x.shape
    n = w.shape[1]
    mod_rows = 1 if sh.shape[1] == 1 else tm
    if tail_width is None:
        n_main = None
        out_specs = pl.BlockSpec((None, tm, tn), lambda b, i, j: (b, i, j))
        out_shape = jax.ShapeDtypeStruct((b, s, n), BF16)
    else:
        assert tail_width == tn
        n_main = (n - tail_width) // tn
        out_specs = [
            pl.BlockSpec((None, tm, tn), lambda b, i, j: (b, i, jnp.minimum(j, n_main - 1))),
            pl.BlockSpec((None, tm, tn), lambda b, i, j: (b, i, 0)),
        ]
        out_shape = [jax.ShapeDtypeStruct((b, s, n - tail_width), BF16),
                     jax.ShapeDtypeStruct((b, s, tail_width), F32)]
    return pl.pallas_call(
        functools.partial(_proj_kernel, n_main=n_main),
        grid=(b, s // tm, n // tn),
        in_specs=[
            pl.BlockSpec((None, tm, d), lambda b, i, j: (b, i, 0)),
            pl.BlockSpec((1, d), lambda b, i, j: (0, 0)),
            _mod_spec(mod_rows, d), _mod_spec(mod_rows, d),
            pl.BlockSpec((d, tn), lambda b, i, j: (0, j)),
        ],
        out_specs=out_specs,
        out_shape=out_shape,
        scratch_shapes=[pltpu.VMEM((tm, d), BF16)],
        compiler_params=_params("arbitrary", "arbitrary", "arbitrary"),
        name="modnorm_proj",
    )(x, g.reshape(1, d), sh, sc, w)


def _kv_variants(t):
    low = lax.broadcasted_iota(jnp.int32, (1, LANES), 1) < HEAD_DIM
    swapped = pltpu.roll(t, HEAD_DIM, 1)
    zero = jnp.zeros_like(t)
    return {
        (0, 0): jnp.where(low, t, zero).astype(BF16), (0, 1): jnp.where(low, zero, swapped).astype(BF16),
        (1, 0): jnp.where(low, swapped, zero).astype(BF16), (1, 1): jnp.where(low, zero, t).astype(BF16),
    }


def _pad_rows(x, rows):
    return jnp.concatenate([x, jnp.zeros((rows - x.shape[0], x.shape[1]), x.dtype)], axis=0)


def _attn_rows(q_rows, k_var, v_var, bias_of):
    r = q_rows.shape[0]
    blocks_per_kv = ATTN_GROUP // 2
    low = lax.broadcasted_iota(jnp.int32, (ATTN_KEY_PAD, LANES), 1) < HEAD_DIM
    ones = (jnp.where(low, 1.0, 0.0).astype(BF16), jnp.where(low, 0.0, 1.0).astype(BF16))
    outs = []
    for h in range(ATTN_KV_HEADS):
        stack = jnp.concatenate(
            [q_rows[:, (h * blocks_per_kv + t) * LANES:(h * blocks_per_kv + t + 1) * LANES]
             for t in range(blocks_per_kv)], axis=0)
        stack = (stack.astype(F32) * (HEAD_DIM ** -0.5)).astype(BF16)
        res = None
        for half in range(2):
            keys = _pad_rows(k_var[h, half], ATTN_KEY_PAD)
            s = lax.dot_general(stack, keys, (((1,), (1,)), ((), ())), preferred_element_type=F32)
            s = s + bias_of(h, half)
            p = jnp.exp(s - jnp.max(s, axis=1, keepdims=True)).astype(BF16)
            rhs = jnp.concatenate([_pad_rows(v_var[h, half], ATTN_KEY_PAD), ones[half]], axis=1)
            part = jnp.dot(p, rhs, preferred_element_type=F32)
            res = part if res is None else res + part
        blk = res[:, :LANES] / res[:, LANES:]
        outs.extend(blk[t * r:(t + 1) * r] for t in range(blocks_per_kv))
    return jnp.concatenate(outs, axis=1)


def _alibi_slopes():
    return 2.0 ** (-8.0 * np.arange(1, ATTN_HEADS + 1) / ATTN_HEADS)


def _attn_bias(dist, valid, sinks):
    q, s = dist.shape
    blocks = ATTN_GROUP // 2
    order = lambda a: np.transpose(a.reshape((ATTN_KV_HEADS, blocks, 2) + a.shape[1:]), (0, 2, 1, 3, 4))
    base = np.where(valid[None], -_alibi_slopes()[:, None, None] * dist[None], -np.inf)
    base = order(base).reshape(ATTN_KV_HEADS, 2, blocks * q, s).astype(np.float32)
    sink = jnp.transpose(sinks.astype(F32).reshape(ATTN_KV_HEADS, blocks, 2), (0, 2, 1))
    sink = jnp.broadcast_to(sink[:, :, :, None, None], (ATTN_KV_HEADS, 2, blocks, q, 1))
    sink = sink.reshape(ATTN_KV_HEADS, 2, blocks * q, 1)
    tail = jnp.full((ATTN_KV_HEADS, 2, blocks * q, ATTN_KEY_PAD - s - 1), -jnp.inf, F32)
    return jnp.concatenate([jnp.asarray(base), sink, tail], axis=-1)


def _attn_prompt_kernel(q_ref, kvc_ref, kvp_ref, bias_ref, o_ref, *, n_chunks):
    i = pl.program_id(1)
    kv = jnp.concatenate([kvp_ref[...], kvc_ref[...]], axis=0)
    k_var = _kv_variants(kv[:, :ATTN_KV_DIM])
    v_var = _kv_variants(kv[:, ATTN_KV_DIM:])
    span = (WINDOW_CHUNKS + 1) * CHUNK
    for cc in range(n_chunks):
        variant = jnp.minimum(i * n_chunks + cc, WINDOW_CHUNKS)
        band = slice(cc * CHUNK, cc * CHUNK + span)
        rows = pl.ds(cc * CHUNK, CHUNK)
        o_ref[rows, :] = _attn_rows(
            q_ref[rows, :], {key: t[band] for key, t in k_var.items()}, {key: t[band] for key, t in v_var.items()},
            lambda h, half: bias_ref[variant, h, half]).astype(o_ref.dtype)


def attn_prompt(pa, pb, sinks, *, qb, out_dtype=BF16):
    b, s, _ = pa.shape
    n_chunks = qb // CHUNK
    span = (WINDOW_CHUNKS + 1) * CHUNK
    qi = np.arange(CHUNK)
    kj = np.arange(span)
    dist = np.abs(qi[:, None] + WINDOW_CHUNKS * CHUNK - kj[None, :]).astype(np.float64)
    bias = jnp.stack([
        _attn_bias(dist, np.broadcast_to(c - WINDOW_CHUNKS + kj[None, :] // CHUNK >= 0, dist.shape), sinks)
        for c in range(WINDOW_CHUNKS + 1)])
    kv_w = 2 * ATTN_KV_DIM
    return pl.pallas_call(
        functools.partial(_attn_prompt_kernel, n_chunks=n_chunks),
        grid=(b, s // qb),
        in_specs=[
            pl.BlockSpec((None, qb, ATTN_Q_DIM), lambda b, i: (b, i, PA_Q // ATTN_Q_DIM)),
            pl.BlockSpec((None, qb, kv_w), lambda b, i: (b, i, PB_K // kv_w)),
            pl.BlockSpec((None, WINDOW, kv_w), lambda b, i: (b, jnp.maximum(i * (qb // WINDOW) - 1, 0), PB_K // kv_w)),
            pl.BlockSpec(bias.shape, lambda b, i: (0,) * bias.ndim),
        ],
        out_specs=pl.BlockSpec((None, qb, ATTN_Q_DIM), lambda b, i: (b, i, 0)),
        out_shape=jax.ShapeDtypeStruct((b, s, ATTN_Q_DIM), out_dtype),
        compiler_params=_params("arbitrary", "arbitrary"),
        name="attn_prompt",
    )(pa, pb, pb, bias)


def _attn_sample_kernel(q_ref, kvn_ref, kc_ref, vc_ref, bias_ref, o_ref):
    kvn = kvn_ref[...]
    k_var = _kv_variants(jnp.concatenate([kc_ref[...], kvn[:, :ATTN_KV_DIM]], axis=0))
    v_var = _kv_variants(jnp.concatenate([vc_ref[...], kvn[:, ATTN_KV_DIM:]], axis=0))
    o_ref[...] = _attn_rows(q_ref[...], k_var, v_var, lambda h, half: bias_ref[h, half]).astype(o_ref.dtype)


def attn_sample(pa, pb, k_cache, v_cache, sinks, *, out_dtype=BF16):
    b, length, _ = pa.shape
    rows = k_cache.shape[1]
    qpos = PAST_LEN + np.arange(length)
    kpos = PAST_LEN - rows + np.arange(rows + length)
    qch, kch = qpos // CHUNK, kpos // CHUNK
    valid = (kch[None, :] <= qch[:, None]) & (kch[None, :] >= qch[:, None] - WINDOW_CHUNKS)
    dist = np.abs(qpos[:, None] - kpos[None, :]).astype(np.float64)
    bias = _attn_bias(dist, valid, sinks)
    kv_w = 2 * ATTN_KV_DIM
    cache = pl.BlockSpec((None, rows, ATTN_KV_DIM), lambda b: (b, 0, 0))
    return pl.pallas_call(
        _attn_sample_kernel,
        grid=(b,),
        in_specs=[
            pl.BlockSpec((None, length, ATTN_Q_DIM), lambda b: (b, 0, PA_Q // ATTN_Q_DIM)),
            pl.BlockSpec((None, length, kv_w), lambda b: (b, 0, PB_K // kv_w)),
            cache, cache,
            pl.BlockSpec(bias.shape, lambda b: (0,) * bias.ndim),
        ],
        out_specs=pl.BlockSpec((None, length, ATTN_Q_DIM), lambda b: (b, 0, 0)),
        out_shape=jax.ShapeDtypeStruct((b, length, ATTN_Q_DIM), out_dtype),
        compiler_params=_params("arbitrary"),
        name="attn_sample",
    )(pa, pb, k_cache, v_cache, bias)


def _causal_conv(pad_ref, prev_ref, cur, w_ref, b_ref, first, q):
    k = w_ref.shape[0]

    @pl.when(first)
    def _():
        pad_ref[pl.ds(0, SUBLANES), :] = prev_ref[...]

    pad_ref[pl.ds(SUBLANES, q), :] = cur
    acc = b_ref[...] + w_ref[k - 1:k, :] * cur
    for t in range(k - 1):
        acc = acc + w_ref[t:t + 1, :] * pad_ref[pl.ds(SUBLANES - (k - 1) + t, q), :]
    pad_ref[pl.ds(0, SUBLANES), :] = pad_ref[pl.ds(q, SUBLANES), :]
    return acc


def _split3(x):
    hi = x.astype(BF16)
    r = x - hi.astype(F32)
    mid = r.astype(BF16)
    lo = (r - mid.astype(F32)).astype(BF16)
    return hi, mid, lo


def _ssd_kernel(x_ref, bc_ref, z_ref, dt_ref, xprev_ref, bcprev_ref, h0_ref,
                wx_ref, bx_ref, wbc_ref, bbc_ref, dtb_ref, alog_ref, dskip_ref, ng_ref,
                y_ref, hout_ref, xpad, bcpad, h_scr, *, q, n_blocks, valid_len):
    i = pl.program_id(1)
    first = i == 0

    @pl.when(first)
    def _():
        h_scr[...] = h0_ref[...]

    xs = _silu(_causal_conv(xpad, xprev_ref, x_ref[...].astype(F32), wx_ref, bx_ref, first, q))
    bc = _silu(_causal_conv(bcpad, bcprev_ref, bc_ref[...].astype(F32), wbc_ref, bbc_ref, first, q))
    xs_b = xs.astype(BF16)

    dt = _softplus(dt_ref[...] + dtb_ref[...])
    if valid_len < q:
        row = lax.broadcasted_iota(jnp.int32, dt.shape, 0)
        dt = jnp.where(row < valid_len, dt, 0.0)
    da = dt * (-jnp.exp(alog_ref[...]))
    r_io = lax.broadcasted_iota(jnp.int32, (q, q), 0)
    c_io = lax.broadcasted_iota(jnp.int32, (q, q), 1)
    causal = c_io <= r_io
    tri = jnp.where(causal, 1.0, 0.0).astype(BF16)
    cum = sum(jnp.dot(tri, part, preferred_element_type=F32) for part in _split3(da))
    cum_t = cum.T
    dt_t = dt.T
    lane = lax.broadcasted_iota(jnp.int32, (1, LANES), 1)
    low_half = lane < SSD_HEAD_DIM

    y_pairs = []
    for g in range(SSD_GROUPS):
        b_g = bc[:, g * SSD_STATE:(g + 1) * SSD_STATE]
        c_g = bc[:, (SSD_GROUPS + g) * SSD_STATE:(SSD_GROUPS + g + 1) * SSD_STATE]
        cb = lax.dot_general(c_g.astype(BF16), b_g.astype(BF16), (((1,), (1,)), ((), ())),
                             preferred_element_type=F32)
        b_t = b_g.T
        for pr in range(g * SSD_HPG // 2, (g + 1) * SSD_HPG // 2):
            x_pair = xs_b[:, pr * LANES:(pr + 1) * LANES]
            h_pair = h_scr[pr]
            h_pair_b = h_pair.astype(BF16)
            ys, states, decays = [], [], []
            for j in (2 * pr, 2 * pr + 1):
                col = jnp.broadcast_to(cum[:, j:j + 1], (q, q))
                row = cum_t[j:j + 1, :]
                decay_in = jnp.exp(jnp.where(causal, col - row, -jnp.inf))
                m = (cb * decay_in * dt_t[j:j + 1, :]).astype(BF16)
                c_scaled = (c_g * jnp.exp(col[:, :SSD_STATE])).astype(BF16)
                ys.append(jnp.dot(m, x_pair, preferred_element_type=F32)
                          + jnp.dot(c_scaled, h_pair_b, preferred_element_type=F32))
                last = row[:, q - 1:q]
                w_row = dt_t[j:j + 1, :] * jnp.exp(last - row)
                states.append(jnp.dot((b_t * w_row).astype(BF16), x_pair, preferred_element_type=F32))
                decays.append(jnp.exp(last))
            h_scr[pr] = (jnp.where(low_half, decays[0], decays[1]) * h_pair
                         + jnp.where(low_half, states[0], states[1]))
            y_pairs.append(jnp.where(low_half, ys[0], ys[1]))
    y = jnp.concatenate(y_pairs, axis=1) + dskip_ref[...] * xs
    y = y * _silu(z_ref[...].astype(F32))
    half = SSD_INNER // SSD_GROUPS
    normed = []
    for g in range(SSD_GROUPS):
        yg = y[:, g * half:(g + 1) * half]
        normed.append(yg * lax.rsqrt(jnp.mean(yg * yg, axis=-1, keepdims=True) + EPS))
    y_ref[...] = (jnp.concatenate(normed, axis=1) * ng_ref[...]).astype(y_ref.dtype)

    @pl.when(i == n_blocks - 1)
    def _():
        hout_ref[...] = h_scr[...]


def ssd_mixer(pa, pb, x_prev, bc_prev, h0, conv_w, conv_b, dt_bias, a_log, d_skip, norm_g, *, q, valid_len=None,
              out_dtype=BF16):
    b, s, _ = pa.shape
    n_blocks = s // q
    valid_len = q if valid_len is None else valid_len
    n_pairs = SSD_HEADS // 2
    pad128 = lambda v: jnp.pad(v.astype(F32), (0, LANES - v.shape[0])).reshape(1, LANES)
    col = lambda width, off: pl.BlockSpec((None, q, width), lambda b, i: (b, i, off // width))
    const = lambda shape: pl.BlockSpec(shape, lambda b, i: (0,) * len(shape))
    per_b = lambda shape: pl.BlockSpec((None,) + shape, lambda b, i: (b,) + (0,) * len(shape))
    h_shape = (n_pairs, SSD_STATE, LANES)
    return pl.pallas_call(
        functools.partial(_ssd_kernel, q=q, n_blocks=n_blocks, valid_len=valid_len),
        grid=(b, n_blocks),
        in_specs=[
            col(SSD_INNER, PA_X), col(SSD_BC_DIM, PA_BC), col(SSD_INNER, PA_Z), col(LANES, PB_DT),
            per_b((SUBLANES, SSD_INNER)), per_b((SUBLANES, SSD_BC_DIM)), per_b(h_shape),
            const((SSD_CONV, SSD_INNER)), const((1, SSD_INNER)),
            const((SSD_CONV, SSD_BC_DIM)), const((1, SSD_BC_DIM)),
            const((1, LANES)), const((1, LANES)), const((1, SSD_INNER)), const((1, SSD_INNER)),
        ],
        out_specs=[
            pl.BlockSpec((None, q, SSD_INNER), lambda b, i: (b, i, 0)),
            per_b(h_shape),
        ],
        out_shape=[
            jax.ShapeDtypeStruct((b, s, SSD_INNER), out_dtype),
            jax.ShapeDtypeStruct((b,) + h_shape, F32),
        ],
        scratch_shapes=[
            pltpu.VMEM((q + SUBLANES, SSD_INNER), F32),
            pltpu.VMEM((q + SUBLANES, SSD_BC_DIM), F32),
            pltpu.VMEM(h_shape, F32),
        ],
        compiler_params=_params("arbitrary", "arbitrary"),
        name="ssd_mixer",
    )(pa, pa, pa, pb, x_prev, bc_prev, h0,
      conv_w[:, :SSD_INNER], conv_b[:SSD_INNER].reshape(1, -1),
      conv_w[:, SSD_INNER:], conv_b[SSD_INNER:].reshape(1, -1),
      pad128(dt_bias), pad128(a_log),
      jnp.repeat(d_skip.astype(F32), SSD_HEAD_DIM).reshape(1, -1), norm_g.astype(F32).reshape(1, -1))


def _pair_state(h):
    b = h.shape[0]
    h = h.astype(F32).reshape(b, SSD_HEADS // 2, 2, SSD_HEAD_DIM, SSD_STATE)
    return h.transpose(0, 1, 4, 2, 3).reshape(b, SSD_HEADS // 2, SSD_STATE, 2 * SSD_HEAD_DIM)


def _unpair_state(h):
    b = h.shape[0]
    h = h.reshape(b, SSD_HEADS // 2, SSD_STATE, 2, SSD_HEAD_DIM)
    return h.transpose(0, 1, 3, 4, 2).reshape(b, SSD_HEADS, SSD_HEAD_DIM, SSD_STATE)


def _tail_rows(prev):
    return jnp.pad(prev.astype(F32), ((0, 0), (SUBLANES - prev.shape[1], 0), (0, 0)))


def _outproj_kernel(x_ref, a_ref, y_ref, gt_ref, wa_ref, wy_ref, o_ref):
    m = (jnp.dot(a_ref[...].astype(BF16), wa_ref[...], preferred_element_type=F32)
         + jnp.dot(y_ref[...].astype(BF16), wy_ref[...], preferred_element_type=F32))
    o_ref[...] = x_ref[...] + gt_ref[...] * m


def mix0_out(x, attn, y, gt, w_out, *, tm):
    b, s, d = x.shape
    da, dy = attn.shape[2], y.shape[2]
    mod_rows = 1 if gt.shape[1] == 1 else tm
    x_spec = pl.BlockSpec((None, tm, d), lambda b, i: (b, i, 0))
    return pl.pallas_call(
        _outproj_kernel,
        grid=(b, s // tm),
        in_specs=[
            x_spec,
            pl.BlockSpec((None, tm, da), lambda b, i: (b, i, 0)),
            pl.BlockSpec((None, tm, dy), lambda b, i: (b, i, 0)),
            _mod_spec(mod_rows, d),
            pl.BlockSpec((da, d), lambda b, i: (0, 0)),
            pl.BlockSpec((dy, d), lambda b, i: (1, 0)),
        ],
        out_specs=x_spec,
        out_shape=jax.ShapeDtypeStruct(x.shape, F32),
        compiler_params=_params("arbitrary", "arbitrary"),
        name="mix0_out",
    )(x, attn, y, gt, w_out, w_out)


def _sconv_kernel(x_ref, gb_ref, gc_ref, xi_ref, prev_ref, gt_ref, cw_ref, w_ref, o_ref, buf_ref, vpad, *, tm):
    first = pl.program_id(1) == 0
    v = gc_ref[...].astype(F32) * xi_ref[...].astype(F32)
    k = cw_ref.shape[0]

    @pl.when(first)
    def _():
        vpad[pl.ds(0, SUBLANES), :] = prev_ref[...]

    vpad[pl.ds(SUBLANES, tm), :] = v
    u = cw_ref[k - 1:k, :] * v
    for t in range(k - 1):
        u = u + cw_ref[t:t + 1, :] * vpad[pl.ds(SUBLANES - (k - 1) + t, tm), :]
    tail = vpad[pl.ds(tm, SUBLANES), :]
    vpad[pl.ds(0, SUBLANES), :] = tail
    buf_ref[...] = tail
    r = (gb_ref[...].astype(F32) * u).astype(BF16)
    o_ref[...] = x_ref[...] + gt_ref[...] * jnp.dot(r, w_ref[...], preferred_element_type=F32)


def sconv_mixer(x, p1, prev, gt, conv_w, w_out, *, tm):
    b, s, d = x.shape
    x_spec = pl.BlockSpec((None, tm, d), lambda b, i: (b, i, 0))
    part = lambda c: pl.BlockSpec((None, tm, d), lambda b, i: (b, i, c))
    buf_spec = pl.BlockSpec((None, SUBLANES, d), lambda b, i: (b, 0, 0))
    return pl.pallas_call(
        functools.partial(_sconv_kernel, tm=tm),
        grid=(b, s // tm),
        in_specs=[
            x_spec, part(0), part(1), part(2), buf_spec, _mod_spec(1, d),
            pl.BlockSpec((SCONV_WIDTH, d), lambda b, i: (0, 0)),
            pl.BlockSpec((d, d), lambda b, i: (0, 0)),
        ],
        out_specs=[x_spec, buf_spec],
        out_shape=[jax.ShapeDtypeStruct(x.shape, F32), jax.ShapeDtypeStruct((b, SUBLANES, d), F32)],
        scratch_shapes=[pltpu.VMEM((tm + SUBLANES, d), F32)],
        compiler_params=_params("arbitrary", "arbitrary"),
        name="sconv_mixer",
    )(x, p1, p1, p1, prev, gt, conv_w.astype(F32), w_out)


def _arrange_w_in0(w):
    i1 = ATTN_Q_DIM
    i2 = i1 + ATTN_KV_DIM
    i3 = i2 + ATTN_KV_DIM
    i4 = i3 + SSD_INNER
    i5 = i4 + SSD_CONV_DIM
    q, k, v, z, xbc, dt = (w[:, a:b] for a, b in ((0, i1), (i1, i2), (i2, i3), (i3, i4), (i4, i5), (i5, w.shape[1])))
    pad = jnp.zeros((w.shape[0], PB_WIDTH - PB_DT - dt.shape[1]), w.dtype)
    return jnp.concatenate([q, z, xbc, k, v, dt, pad], axis=1).astype(BF16)


def _split_mod(mod):
    return [m[:, None, :] for m in jnp.split(mod, N_MOD, axis=-1)]


def _per_token(m, length):
    b, _, d = m.shape
    return jnp.broadcast_to(m, (b, length, d)).reshape(1, b * length, d)


def _trunk(x, mods, weights, cache, *, tm, tm_proj, tm_mix, q_ssd, qb_attn):
    b, s, d = x.shape
    flat = cache is not None
    if flat:
        as_rows = lambda t: t.reshape(1, b * s, t.shape[-1])
        mod_of = lambda m: _per_token(m, s)
    else:
        as_rows = lambda t: t
        mod_of = lambda m: m
    unrow = lambda t: t.reshape(b, s, t.shape[-1])
    w = weights

    def ffn(xr, l, k, sh, sc, gt, final_g=None):
        return ffn_half(xr, w["norm_g"][l, 2 * k], mod_of(sh), mod_of(sc), mod_of(gt),
                        w["ffn_gate"], w["ffn_up"], w["ffn_down"], l, k, final_g, tm=tm)

    sh1, sc1, g1, sh2, sc2, g2, sh3, sc3, g3 = mods[0]
    xr = ffn(as_rows(x), 0, 0, sh1, sc1, g1)
    pa, pb = modnorm_proj(xr, w["norm_g"][0, 1], mod_of(sh2), mod_of(sc2), w["w_in0"], tm=tm_proj,
                          tail_width=PB_WIDTH)
    pa, pb = unrow(pa), unrow(pb)
    if cache is None:
        attn = attn_prompt(pa, pb, w["sinks"], qb=qb_attn)
        x_prev = jnp.zeros((b, SUBLANES, SSD_INNER), F32)
        bc_prev = jnp.zeros((b, SUBLANES, SSD_BC_DIM), F32)
        h0 = jnp.zeros((b, SSD_HEADS // 2, SSD_STATE, LANES), F32)
        y, h_new = ssd_mixer(pa, pb, x_prev, bc_prev, h0, *w["ssd"], q=q_ssd)
        new_kv = pb[:, s - WINDOW:]
        sconv_prev = jnp.zeros((b, SUBLANES, d), F32)
    else:
        k_cache, v_cache, h_state, conv_prev, sconv_state = cache
        rows = k_cache.shape[1]
        attn = attn_sample(pa, pb, k_cache.reshape(b, rows, ATTN_KV_DIM), v_cache.reshape(b, rows, ATTN_KV_DIM),
                           w["sinks"])
        tail = _tail_rows(conv_prev)
        pad_rows = lambda t: jnp.pad(t, ((0, 0), (0, q_ssd - s), (0, 0)))
        y, h_new = ssd_mixer(pad_rows(pa), pad_rows(pb), tail[:, :, :SSD_INNER], tail[:, :, SSD_INNER:],
                             _pair_state(h_state), *w["ssd"], q=q_ssd, valid_len=s)
        y = y[:, :s]
        new_kv = pb
        sconv_prev = _tail_rows(sconv_state)
    xbc_raw = pa[:, s - (SSD_CONV - 1):, PA_X:PA_X + SSD_CONV_DIM].astype(F32)
    kv_shape = (b, new_kv.shape[1], ATTN_KV_HEADS, HEAD_DIM)
    states0 = (new_kv[:, :, PB_K:PB_K + ATTN_KV_DIM].reshape(kv_shape),
               new_kv[:, :, PB_V:PB_V + ATTN_KV_DIM].reshape(kv_shape), _unpair_state(h_new), xbc_raw)
    xr = mix0_out(xr, as_rows(attn), as_rows(y), mod_of(g2), w["w_out0"], tm=tm_mix)
    xr = ffn(xr, 0, 1, sh3, sc3, g3)

    sh1, sc1, g1, sh2, sc2, g2, sh3, sc3, g3 = mods[1]
    xr = ffn(xr, 1, 0, sh1, sc1, g1)
    p1 = unrow(modnorm_proj(xr, w["norm_g"][1, 1], mod_of(sh2), mod_of(sc2), w["w_in1"], tm=tm_proj))
    xm, buf = sconv_mixer(unrow(xr), p1, sconv_prev, g2, w["sconv_w"], w["w_out1"], tm=min(tm_mix, s))
    sconv_new = buf[:, SUBLANES - (SCONV_WIDTH - 1):]
    y_out = unrow(ffn(as_rows(xm), 1, 1, sh3, sc3, g3, final_g=w["final_g"]))
    return y_out, states0, sconv_new


def kernel(x_prompt, x_sample, c_prompt, c_sample, cache_swa_k, cache_swa_v, state_ssd, state_ssd_conv, state_sconv, norm_g, w_ada, b_ada, w_ffn_gate, w_ffn_up, w_ffn_down, w_in_mix0, w_out_mix0, attn_sinks, ssd_conv_w, ssd_conv_b, ssd_dt_bias, ssd_a_log, ssd_d, ssd_norm_g, w_in_mix1, sconv_w, w_out_mix1, final_norm_g):
    bp, bs = c_prompt.shape[0], c_sample.shape[0]
    c_all = jnp.concatenate([c_prompt, c_sample], axis=0)
    c_rows = -(-c_all.shape[0] // SUBLANES) * SUBLANES
    c_all = jnp.pad(c_all, ((0, c_rows - c_all.shape[0]), (0, 0)))
    mod = ada_modulation(c_all, w_ada, b_ada)
    mods_p = [_split_mod(mod[l, :bp]) for l in range(mod.shape[0])]
    mods_s = [_split_mod(mod[l, bp:bp + bs]) for l in range(mod.shape[0])]

    weights = {
        "norm_g": norm_g,
        "ffn_gate": w_ffn_gate.astype(BF16), "ffn_up": w_ffn_up.astype(BF16), "ffn_down": w_ffn_down.astype(BF16),
        "w_in0": _arrange_w_in0(w_in_mix0[0]), "w_out0": w_out_mix0[0].astype(BF16),
        "sinks": attn_sinks[0],
        "ssd": (ssd_conv_w[0].astype(F32), ssd_conv_b[0].astype(F32), ssd_dt_bias[0], ssd_a_log[0], ssd_d[0],
                ssd_norm_g[0]),
        "w_in1": w_in_mix1[0].astype(BF16), "sconv_w": sconv_w[0], "w_out1": w_out_mix1[0].astype(BF16),
        "final_g": final_norm_g,
    }
    seq = x_prompt.shape[1]
    y_p, (k_p, v_p, h_p, cv_p), sc_p = _trunk(
        x_prompt, mods_p, weights, None, tm=min(512, seq), tm_proj=min(1024, seq), tm_mix=min(256, seq),
        q_ssd=min(256, seq), qb_attn=min(256, seq))
    n_s = x_sample.shape[0] * x_sample.shape[1]
    cache = (cache_swa_k[0], cache_swa_v[0], state_ssd[0], state_ssd_conv[0], state_sconv[0])
    y_s, (k_s, v_s, h_s, cv_s), sc_s = _trunk(
        x_sample, mods_s, weights, cache, tm=n_s, tm_proj=n_s, tm_mix=n_s, q_ssd=LANES, qb_attn=None)
    stack = lambda t: t[None]
    return (y_p, y_s, stack(k_p), stack(v_p), stack(h_p), stack(cv_p), stack(sc_p),
            stack(k_s), stack(v_s), stack(h_s), stack(cv_s), stack(sc_s))
```

```python
import functools

import numpy as np
import jax
import jax.numpy as jnp
from jax import lax
from jax.experimental import pallas as pl
from jax.experimental.pallas import tpu as pltpu

F32 = jnp.float32
BF16 = jnp.bfloat16

EPS = 1e-6
CHUNK = 64
PAST_LEN = 2048
ATTN_HEADS = 16
ATTN_KV_HEADS = 2
ATTN_GROUP = ATTN_HEADS // ATTN_KV_HEADS
HEAD_DIM = 64
WINDOW = 128
WINDOW_CHUNKS = WINDOW // CHUNK
ATTN_Q_DIM = ATTN_HEADS * HEAD_DIM
ATTN_KV_DIM = ATTN_KV_HEADS * HEAD_DIM
SSD_HEADS = 16
SSD_HEAD_DIM = 64
SSD_INNER = SSD_HEADS * SSD_HEAD_DIM
SSD_STATE = 128
SSD_GROUPS = 2
SSD_HPG = SSD_HEADS // SSD_GROUPS
SSD_CONV = 4
SSD_BC_DIM = 2 * SSD_GROUPS * SSD_STATE
SSD_CONV_DIM = SSD_INNER + SSD_BC_DIM
SCONV_WIDTH = 3
N_MOD = 9

LANES = 128
SUBLANES = 8
MXU_DIM = 256
VMEM_LIMIT = 56 * 1024 * 1024

PROJ_TN = 512
PA_Q = 0
PA_Z = PA_Q + ATTN_Q_DIM
PA_X = PA_Z + SSD_INNER
PA_BC = PA_X + SSD_INNER
PA_WIDTH = PA_BC + SSD_BC_DIM
PB_K = 0
PB_V = PB_K + ATTN_KV_DIM
PB_DT = PB_V + ATTN_KV_DIM
PB_WIDTH = PROJ_TN
ATTN_KEY_PAD = MXU_DIM
assert HEAD_DIM * 2 == LANES and ATTN_KV_DIM == LANES and PA_WIDTH % PROJ_TN == 0


def _params(*sem):
    return pltpu.CompilerParams(dimension_semantics=sem, vmem_limit_bytes=VMEM_LIMIT)


def _silu(x):
    return x / (1.0 + jnp.exp(-x))


def _softplus(x):
    return jnp.maximum(x, 0.0) + jnp.log(1.0 + jnp.exp(-jnp.abs(x)))


NORM_ROWS = 2 * SUBLANES


def _modnorm_store(h_ref, x_ref, g_ref, sc_ref, sh_ref, rs_scr):
    tm, d = x_ref.shape
    for c in range(tm // SUBLANES):
        r = pl.ds(c * SUBLANES, SUBLANES)
        x = x_ref[r, :]
        ms = jnp.mean(x * x, axis=-1, keepdims=True)
        rs_scr[r, :] = jnp.broadcast_to(lax.rsqrt(ms + EPS), (SUBLANES, LANES))
    per_row = sc_ref.shape[0] != 1
    scale = None if per_row else g_ref[...] * (1.0 + sc_ref[...])

    def body(c, carry):
        r = pl.ds(pl.multiple_of(c * NORM_ROWS, NORM_ROWS), NORM_ROWS)
        rs = jnp.concatenate([rs_scr[r, :]] * (d // LANES), axis=1)
        y = x_ref[r, :] * rs
        if per_row:
            h = y * (g_ref[...] * (1.0 + sc_ref[r, :])) + sh_ref[r, :]
        else:
            h = y * scale + sh_ref[...]
        h_ref[r, :] = h.astype(BF16)
        return carry

    lax.fori_loop(0, tm // NORM_ROWS, body, 0, unroll=2 if tm >= 2 * NORM_ROWS else 1)


def _mod_spec(mod_rows, d):
    if mod_rows == 1:
        return pl.BlockSpec((None, 1, d), lambda b, i, *_: (b, 0, 0))
    return pl.BlockSpec((None, mod_rows, d), lambda b, i, *_: (b, i, 0))


def _ada_kernel(c_ref, w_ref, b_ref, o_ref):
    c = c_ref[...]
    s = _silu(c).astype(BF16)
    o_ref[...] = jnp.dot(s, w_ref[...].astype(BF16), preferred_element_type=F32) + b_ref[...]


def ada_modulation(c, w_ada, b_ada, tn=1024):
    n_layers, d, n = w_ada.shape
    r = c.shape[0]
    return pl.pallas_call(
        _ada_kernel,
        grid=(n_layers, n // tn),
        in_specs=[
            pl.BlockSpec((r, d), lambda l, j: (0, 0)),
            pl.BlockSpec((None, d, tn), lambda l, j: (l, 0, j)),
            pl.BlockSpec((None, 1, tn), lambda l, j: (l, 0, j)),
        ],
        out_specs=pl.BlockSpec((None, r, tn), lambda l, j: (l, 0, j)),
        out_shape=jax.ShapeDtypeStruct((n_layers, r, n), F32),
        compiler_params=_params("arbitrary", "arbitrary"),
        name="ada_modulation",
    )(c, w_ada, b_ada.reshape(n_layers, 1, n))


def _ffn_kernel(x_ref, g_ref, sh_ref, sc_ref, gt_ref, wg_ref, wu_ref, wd_ref, *rest, n_ff, final):
    if final:
        fg_ref, o_ref, h_scr, acc, rs_scr = rest
    else:
        o_ref, h_scr, acc, rs_scr = rest
    j = pl.program_id(2)

    @pl.when(j == 0)
    def _():
        _modnorm_store(h_scr, x_ref, g_ref, sc_ref, sh_ref, rs_scr)
        acc[...] = jnp.zeros_like(acc)

    h = h_scr[...]
    g = jnp.dot(h, wg_ref[...], preferred_element_type=F32)
    u = jnp.dot(h, wu_ref[...], preferred_element_type=F32)
    a = (_silu(g) * u).astype(BF16)
    acc[...] += jnp.dot(a, wd_ref[...], preferred_element_type=F32)

    @pl.when(j == n_ff - 1)
    def _():
        xn = x_ref[...] + 0.5 * gt_ref[...] * acc[...]
        if final:
            xn = xn * lax.rsqrt(jnp.mean(xn * xn, axis=-1, keepdims=True) + EPS) * fg_ref[...]
        o_ref[...] = xn


def ffn_half(x, g, sh, sc, gt, wg, wu, wd, layer, half, final_g=None, *, tm, tf=512):
    b, s, d = x.shape
    f = wg.shape[-1]
    n_ff = f // tf
    mod_rows = 1 if sh.shape[1] == 1 else tm
    final = final_g is not None
    x_spec = pl.BlockSpec((None, tm, d), lambda b, i, j: (b, i, 0))
    in_specs = [
        x_spec,
        pl.BlockSpec((1, d), lambda b, i, j: (0, 0)),
        _mod_spec(mod_rows, d), _mod_spec(mod_rows, d), _mod_spec(mod_rows, d),
        pl.BlockSpec((None, None, d, tf), lambda b, i, j: (layer, half, 0, j)),
        pl.BlockSpec((None, None, d, tf), lambda b, i, j: (layer, half, 0, j)),
        pl.BlockSpec((None, None, tf, d), lambda b, i, j: (layer, half, j, 0)),
    ]
    args = [x, g.reshape(1, d), sh, sc, gt, wg, wu, wd]
    if final:
        in_specs.append(pl.BlockSpec((1, d), lambda b, i, j: (0, 0)))
        args.append(final_g.reshape(1, d))
    return pl.pallas_call(
        functools.partial(_ffn_kernel, n_ff=n_ff, final=final),
        grid=(b, s // tm, n_ff),
        in_specs=in_specs,
        out_specs=x_spec,
        out_shape=jax.ShapeDtypeStruct(x.shape, F32),
        scratch_shapes=[pltpu.VMEM((tm, d), BF16), pltpu.VMEM((tm, d), F32), pltpu.VMEM((tm, LANES), F32)],
        compiler_params=_params("arbitrary", "arbitrary", "arbitrary"),
        name="ffn_half_final" if final else "ffn_half",
    )(*args)


def _proj_kernel(x_ref, g_ref, sh_ref, sc_ref, w_ref, *rest, n_main):
    j = pl.program_id(2)
    h_scr, rs_scr = rest[-2:]

    @pl.when(j == 0)
    def _():
        _modnorm_store(h_scr, x_ref, g_ref, sc_ref, sh_ref, rs_scr)

    r = jnp.dot(h_scr[...], w_ref[...], preferred_element_type=F32)
    if n_main is None:
        rest[0][...] = r.astype(rest[0].dtype)
    else:
        main_ref, tail_ref = rest[0], rest[1]

        @pl.when(j < n_main)
        def _():
            main_ref[...] = r.astype(main_ref.dtype)

        @pl.when(j >= n_main)
        def _():
            tail_ref[...] = r


def modnorm_proj(x, g, sh, sc, w, *, tm, tn=PROJ_TN, tail_width=None):
    b, s, d = x.shape
    n = w.shape[1]
    mod_rows = 1 if sh.shape[1] == 1 else tm
    if tail_width is None:
        n_main = None
        out_specs = pl.BlockSpec((None, tm, tn), lambda b, i, j: (b, i, j))
        out_shape = jax.ShapeDtypeStruct((b, s, n), BF16)
    else:
        assert tail_width == tn
        n_main = (n - tail_width) // tn
        out_specs = [
            pl.BlockSpec((None, tm, tn), lambda b, i, j: (b, i, jnp.minimum(j, n_main - 1))),
            pl.BlockSpec((None, tm, tn), lambda b, i, j: (b, i, 0)),
        ]
        out_shape = [jax.ShapeDtypeStruct((b, s, n - tail_width), BF16),
                     jax.ShapeDtypeStruct((b, s, tail_width), F32)]
    return pl.pallas_call(
        functools.partial(_proj_kernel, n_main=n_main),
        grid=(b, s // tm, n // tn),
        in_specs=[
            pl.BlockSpec((None, tm, d), lambda b, i, j: (b, i, 0)),
            pl.BlockSpec((1, d), lambda b, i, j: (0, 0)),
            _mod_spec(mod_rows, d), _mod_spec(mod_rows, d),
            pl.BlockSpec((d, tn), lambda b, i, j: (0, j)),
        ],
        out_specs=out_specs,
        out_shape=out_shape,
        scratch_shapes=[pltpu.VMEM((tm, d), BF16), pltpu.VMEM((tm, LANES), F32)],
        compiler_params=_params("arbitrary", "arbitrary", "arbitrary"),
        name="modnorm_proj",
    )(x, g.reshape(1, d), sh, sc, w)


def _kv_variants(t):
    low = lax.broadcasted_iota(jnp.int32, (1, LANES), 1) < HEAD_DIM
    swapped = pltpu.roll(t, HEAD_DIM, 1)
    zero = jnp.zeros_like(t)
    return {
        (0, 0): jnp.where(low, t, zero).astype(BF16), (0, 1): jnp.where(low, zero, swapped).astype(BF16),
        (1, 0): jnp.where(low, swapped, zero).astype(BF16), (1, 1): jnp.where(low, zero, t).astype(BF16),
    }


def _pad_rows(x, rows):
    return jnp.concatenate([x, jnp.zeros((rows - x.shape[0], x.shape[1]), x.dtype)], axis=0)


def _attn_rows(q_rows, k_var, v_var, bias_of):
    r = q_rows.shape[0]
    blocks_per_kv = ATTN_GROUP // 2
    low = lax.broadcasted_iota(jnp.int32, (ATTN_KEY_PAD, LANES), 1) < HEAD_DIM
    ones = (jnp.where(low, 1.0, 0.0).astype(BF16), jnp.where(low, 0.0, 1.0).astype(BF16))
    outs = []
    for h in range(ATTN_KV_HEADS):
        stack = jnp.concatenate(
            [q_rows[:, (h * blocks_per_kv + t) * LANES:(h * blocks_per_kv + t + 1) * LANES]
             for t in range(blocks_per_kv)], axis=0)
        stack = (stack.astype(F32) * (HEAD_DIM ** -0.5)).astype(BF16)
        res = None
        for half in range(2):
            keys = _pad_rows(k_var[h, half], ATTN_KEY_PAD)
            s = lax.dot_general(stack, keys, (((1,), (1,)), ((), ())), preferred_element_type=F32)
            s = s + bias_of(h, half)
            p = jnp.exp(s - jnp.max(s, axis=1, keepdims=True)).astype(BF16)
            rhs = jnp.concatenate([_pad_rows(v_var[h, half], ATTN_KEY_PAD), ones[half]], axis=1)
            part = jnp.dot(p, rhs, preferred_element_type=F32)
            res = part if res is None else res + part
        blk = res[:, :LANES] / res[:, LANES:]
        outs.extend(blk[t * r:(t + 1) * r] for t in range(blocks_per_kv))
    return jnp.concatenate(outs, axis=1)


def _alibi_slopes():
    return 2.0 ** (-8.0 * np.arange(1, ATTN_HEADS + 1) / ATTN_HEADS)


def _attn_bias(dist, valid, sinks):
    q, s = dist.shape
    blocks = ATTN_GROUP // 2
    order = lambda a: np.transpose(a.reshape((ATTN_KV_HEADS, blocks, 2) + a.shape[1:]), (0, 2, 1, 3, 4))
    base = np.where(valid[None], -_alibi_slopes()[:, None, None] * dist[None], -np.inf)
    base = order(base).reshape(ATTN_KV_HEADS, 2, blocks * q, s).astype(np.float32)
    sink = jnp.transpose(sinks.astype(F32).reshape(ATTN_KV_HEADS, blocks, 2), (0, 2, 1))
    sink = jnp.broadcast_to(sink[:, :, :, None, None], (ATTN_KV_HEADS, 2, blocks, q, 1))
    sink = sink.reshape(ATTN_KV_HEADS, 2, blocks * q, 1)
    tail = jnp.full((ATTN_KV_HEADS, 2, blocks * q, ATTN_KEY_PAD - s - 1), -jnp.inf, F32)
    return jnp.concatenate([jnp.asarray(base), sink, tail], axis=-1)


def _attn_prompt_kernel(q_ref, kvc_ref, kvp_ref, bias_ref, o_ref, *, n_chunks):
    i = pl.program_id(1)
    kv = jnp.concatenate([kvp_ref[...], kvc_ref[...]], axis=0)
    k_var = _kv_variants(kv[:, :ATTN_KV_DIM])
    v_var = _kv_variants(kv[:, ATTN_KV_DIM:])
    span = (WINDOW_CHUNKS + 1) * CHUNK
    for cc in range(n_chunks):
        variant = jnp.minimum(i * n_chunks + cc, WINDOW_CHUNKS)
        band = slice(cc * CHUNK, cc * CHUNK + span)
        rows = pl.ds(cc * CHUNK, CHUNK)
        o_ref[rows, :] = _attn_rows(
            q_ref[rows, :], {key: t[band] for key, t in k_var.items()}, {key: t[band] for key, t in v_var.items()},
            lambda h, half: bias_ref[variant, h, half]).astype(o_ref.dtype)


def attn_prompt(pa, pb, sinks, *, qb, out_dtype=BF16):
    b, s, _ = pa.shape
    n_chunks = qb // CHUNK
    span = (WINDOW_CHUNKS + 1) * CHUNK
    qi = np.arange(CHUNK)
    kj = np.arange(span)
    dist = np.abs(qi[:, None] + WINDOW_CHUNKS * CHUNK - kj[None, :]).astype(np.float64)
    bias = jnp.stack([
        _attn_bias(dist, np.broadcast_to(c - WINDOW_CHUNKS + kj[None, :] // CHUNK >= 0, dist.shape), sinks)
        for c in range(WINDOW_CHUNKS + 1)])
    kv_w = 2 * ATTN_KV_DIM
    return pl.pallas_call(
        functools.partial(_attn_prompt_kernel, n_chunks=n_chunks),
        grid=(b, s // qb),
        in_specs=[
            pl.BlockSpec((None, qb, ATTN_Q_DIM), lambda b, i: (b, i, PA_Q // ATTN_Q_DIM)),
            pl.BlockSpec((None, qb, kv_w), lambda b, i: (b, i, PB_K // kv_w)),
            pl.BlockSpec((None, WINDOW, kv_w), lambda b, i: (b, jnp.maximum(i * (qb // WINDOW) - 1, 0), PB_K // kv_w)),
            pl.BlockSpec(bias.shape, lambda b, i: (0,) * bias.ndim),
        ],
        out_specs=pl.BlockSpec((None, qb, ATTN_Q_DIM), lambda b, i: (b, i, 0)),
        out_shape=jax.ShapeDtypeStruct((b, s, ATTN_Q_DIM), out_dtype),
        compiler_params=_params("arbitrary", "arbitrary"),
        name="attn_prompt",
    )(pa, pb, pb, bias)


def _attn_sample_kernel(q_ref, kvn_ref, kc_ref, vc_ref, bias_ref, o_ref):
    kvn = kvn_ref[...]
    k_var = _kv_variants(jnp.concatenate([kc_ref[...], kvn[:, :ATTN_KV_DIM]], axis=0))
    v_var = _kv_variants(jnp.concatenate([vc_ref[...], kvn[:, ATTN_KV_DIM:]], axis=0))
    o_ref[...] = _attn_rows(q_ref[...], k_var, v_var, lambda h, half: bias_ref[h, half]).astype(o_ref.dtype)


def attn_sample(pa, pb, k_cache, v_cache, sinks, *, out_dtype=BF16):
    b, length, _ = pa.shape
    rows = k_cache.shape[1]
    qpos = PAST_LEN + np.arange(length)
    kpos = PAST_LEN - rows + np.arange(rows + length)
    qch, kch = qpos // CHUNK, kpos // CHUNK
    valid = (kch[None, :] <= qch[:, None]) & (kch[None, :] >= qch[:, None] - WINDOW_CHUNKS)
    dist = np.abs(qpos[:, None] - kpos[None, :]).astype(np.float64)
    bias = _attn_bias(dist, valid, sinks)
    kv_w = 2 * ATTN_KV_DIM
    cache = pl.BlockSpec((None, rows, ATTN_KV_DIM), lambda b: (b, 0, 0))
    return pl.pallas_call(
        _attn_sample_kernel,
        grid=(b,),
        in_specs=[
            pl.BlockSpec((None, length, ATTN_Q_DIM), lambda b: (b, 0, PA_Q // ATTN_Q_DIM)),
            pl.BlockSpec((None, length, kv_w), lambda b: (b, 0, PB_K // kv_w)),
            cache, cache,
            pl.BlockSpec(bias.shape, lambda b: (0,) * bias.ndim),
        ],
        out_specs=pl.BlockSpec((None, length, ATTN_Q_DIM), lambda b: (b, 0, 0)),
        out_shape=jax.ShapeDtypeStruct((b, length, ATTN_Q_DIM), out_dtype),
        compiler_params=_params("arbitrary"),
        name="attn_sample",
    )(pa, pb, k_cache, v_cache, bias)


def _causal_conv(pad_ref, prev_ref, cur, w_ref, b_ref, first, q):
    k = w_ref.shape[0]

    @pl.when(first)
    def _():
        pad_ref[pl.ds(0, SUBLANES), :] = prev_ref[...]

    pad_ref[pl.ds(SUBLANES, q), :] = cur
    acc = b_ref[...] + w_ref[k - 1:k, :] * cur
    for t in range(k - 1):
        acc = acc + w_ref[t:t + 1, :] * pad_ref[pl.ds(SUBLANES - (k - 1) + t, q), :]
    pad_ref[pl.ds(0, SUBLANES), :] = pad_ref[pl.ds(q, SUBLANES), :]
    return acc


def _split3(x):
    hi = x.astype(BF16)
    r = x - hi.astype(F32)
    mid = r.astype(BF16)
    lo = (r - mid.astype(F32)).astype(BF16)
    return hi, mid, lo


def _ssd_kernel(x_ref, bc_ref, z_ref, dt_ref, xprev_ref, bcprev_ref, h0_ref,
                wx_ref, bx_ref, wbc_ref, bbc_ref, dtb_ref, alog_ref, dskip_ref, ng_ref,
                y_ref, hout_ref, xpad, bcpad, h_scr, *, q, n_blocks, valid_len):
    i = pl.program_id(1)
    first = i == 0

    @pl.when(first)
    def _():
        h_scr[...] = h0_ref[...]

    xs = _silu(_causal_conv(xpad, xprev_ref, x_ref[...].astype(F32), wx_ref, bx_ref, first, q))
    bc = _silu(_causal_conv(bcpad, bcprev_ref, bc_ref[...].astype(F32), wbc_ref, bbc_ref, first, q))
    xs_b = xs.astype(BF16)

    dt = _softplus(dt_ref[...] + dtb_ref[...])
    if valid_len < q:
        row = lax.broadcasted_iota(jnp.int32, dt.shape, 0)
        dt = jnp.where(row < valid_len, dt, 0.0)
    da = dt * (-jnp.exp(alog_ref[...]))
    r_io = lax.broadcasted_iota(jnp.int32, (q, q), 0)
    c_io = lax.broadcasted_iota(jnp.int32, (q, q), 1)
    causal = c_io <= r_io
    tri = jnp.where(causal, 1.0, 0.0).astype(BF16)
    cum = sum(jnp.dot(tri, part, preferred_element_type=F32) for part in _split3(da))
    cum_t = cum.T
    dt_t = dt.T
    lane = lax.broadcasted_iota(jnp.int32, (1, LANES), 1)
    low_half = lane < SSD_HEAD_DIM

    y_pairs = []
    for g in range(SSD_GROUPS):
        b_g = bc[:, g * SSD_STATE:(g + 1) * SSD_STATE]
        c_g = bc[:, (SSD_GROUPS + g) * SSD_STATE:(SSD_GROUPS + g + 1) * SSD_STATE]
        cb = lax.dot_general(c_g.astype(BF16), b_g.astype(BF16), (((1,), (1,)), ((), ())),
                             preferred_element_type=F32)
        b_t = b_g.T
        for pr in range(g * SSD_HPG // 2, (g + 1) * SSD_HPG // 2):
            x_pair = xs_b[:, pr * LANES:(pr + 1) * LANES]
            h_pair = h_scr[pr]
            h_pair_b = h_pair.astype(BF16)
            ys, states, decays = [], [], []
            for j in (2 * pr, 2 * pr + 1):
                col = jnp.broadcast_to(cum[:, j:j + 1], (q, q))
                row = cum_t[j:j + 1, :]
                decay_in = jnp.exp(jnp.where(causal, col - row, -jnp.inf))
                m = (cb * decay_in * dt_t[j:j + 1, :]).astype(BF16)
                c_scaled = (c_g * jnp.exp(col[:, :SSD_STATE])).astype(BF16)
                ys.append(jnp.dot(m, x_pair, preferred_element_type=F32)
                          + jnp.dot(c_scaled, h_pair_b, preferred_element_type=F32))
                last = row[:, q - 1:q]
                w_row = dt_t[j:j + 1, :] * jnp.exp(last - row)
                states.append(jnp.dot((b_t * w_row).astype(BF16), x_pair, preferred_element_type=F32))
                decays.append(jnp.exp(last))
            h_scr[pr] = (jnp.where(low_half, decays[0], decays[1]) * h_pair
                         + jnp.where(low_half, states[0], states[1]))
            y_pairs.append(jnp.where(low_half, ys[0], ys[1]))
    y = jnp.concatenate(y_pairs, axis=1) + dskip_ref[...] * xs
    y = y * _silu(z_ref[...].astype(F32))
    half = SSD_INNER // SSD_GROUPS
    normed = []
    for g in range(SSD_GROUPS):
        yg = y[:, g * half:(g + 1) * half]
        normed.append(yg * lax.rsqrt(jnp.mean(yg * yg, axis=-1, keepdims=True) + EPS))
    y_ref[...] = (jnp.concatenate(normed, axis=1) * ng_ref[...]).astype(y_ref.dtype)

    @pl.when(i == n_blocks - 1)
    def _():
        hout_ref[...] = h_scr[...]


def ssd_mixer(pa, pb, x_prev, bc_prev, h0, conv_w, conv_b, dt_bias, a_log, d_skip, norm_g, *, q, valid_len=None,
              out_dtype=BF16):
    b, s, _ = pa.shape
    n_blocks = s // q
    valid_len = q if valid_len is None else valid_len
    n_pairs = SSD_HEADS // 2
    pad128 = lambda v: jnp.pad(v.astype(F32), (0, LANES - v.shape[0])).reshape(1, LANES)
    col = lambda width, off: pl.BlockSpec((None, q, width), lambda b, i: (b, i, off // width))
    const = lambda shape: pl.BlockSpec(shape, lambda b, i: (0,) * len(shape))
    per_b = lambda shape: pl.BlockSpec((None,) + shape, lambda b, i: (b,) + (0,) * len(shape))
    h_shape = (n_pairs, SSD_STATE, LANES)
    return pl.pallas_call(
        functools.partial(_ssd_kernel, q=q, n_blocks=n_blocks, valid_len=valid_len),
        grid=(b, n_blocks),
        in_specs=[
            col(SSD_INNER, PA_X), col(SSD_BC_DIM, PA_BC), col(SSD_INNER, PA_Z), col(LANES, PB_DT),
            per_b((SUBLANES, SSD_INNER)), per_b((SUBLANES, SSD_BC_DIM)), per_b(h_shape),
            const((SSD_CONV, SSD_INNER)), const((1, SSD_INNER)),
            const((SSD_CONV, SSD_BC_DIM)), const((1, SSD_BC_DIM)),
            const((1, LANES)), const((1, LANES)), const((1, SSD_INNER)), const((1, SSD_INNER)),
        ],
        out_specs=[
            pl.BlockSpec((None, q, SSD_INNER), lambda b, i: (b, i, 0)),
            per_b(h_shape),
        ],
        out_shape=[
            jax.ShapeDtypeStruct((b, s, SSD_INNER), out_dtype),
            jax.ShapeDtypeStruct((b,) + h_shape, F32),
        ],
        scratch_shapes=[
            pltpu.VMEM((q + SUBLANES, SSD_INNER), F32),
            pltpu.VMEM((q + SUBLANES, SSD_BC_DIM), F32),
            pltpu.VMEM(h_shape, F32),
        ],
        compiler_params=_params("arbitrary", "arbitrary"),
        name="ssd_mixer",
    )(pa, pa, pa, pb, x_prev, bc_prev, h0,
      conv_w[:, :SSD_INNER], conv_b[:SSD_INNER].reshape(1, -1),
      conv_w[:, SSD_INNER:], conv_b[SSD_INNER:].reshape(1, -1),
      pad128(dt_bias), pad128(a_log),
      jnp.repeat(d_skip.astype(F32), SSD_HEAD_DIM).reshape(1, -1), norm_g.astype(F32).reshape(1, -1))


def _pair_state(h):
    b = h.shape[0]
    h = h.astype(F32).reshape(b, SSD_HEADS // 2, 2, SSD_HEAD_DIM, SSD_STATE)
    return h.transpose(0, 1, 4, 2, 3).reshape(b, SSD_HEADS // 2, SSD_STATE, 2 * SSD_HEAD_DIM)


def _unpair_state(h):
    b = h.shape[0]
    h = h.reshape(b, SSD_HEADS // 2, SSD_STATE, 2, SSD_HEAD_DIM)
    return h.transpose(0, 1, 3, 4, 2).reshape(b, SSD_HEADS, SSD_HEAD_DIM, SSD_STATE)


def _tail_rows(prev):
    return jnp.pad(prev.astype(F32), ((0, 0), (SUBLANES - prev.shape[1], 0), (0, 0)))


def _outproj_kernel(x_ref, a_ref, y_ref, gt_ref, wa_ref, wy_ref, o_ref):
    m = (jnp.dot(a_ref[...].astype(BF16), wa_ref[...], preferred_element_type=F32)
         + jnp.dot(y_ref[...].astype(BF16), wy_ref[...], preferred_element_type=F32))
    o_ref[...] = x_ref[...] + gt_ref[...] * m


def mix0_out(x, attn, y, gt, w_out, *, tm):
    b, s, d = x.shape
    da, dy = attn.shape[2], y.shape[2]
    mod_rows = 1 if gt.shape[1] == 1 else tm
    x_spec = pl.BlockSpec((None, tm, d), lambda b, i: (b, i, 0))
    return pl.pallas_call(
        _outproj_kernel,
        grid=(b, s // tm),
        in_specs=[
            x_spec,
            pl.BlockSpec((None, tm, da), lambda b, i: (b, i, 0)),
            pl.BlockSpec((None, tm, dy), lambda b, i: (b, i, 0)),
            _mod_spec(mod_rows, d),
            pl.BlockSpec((da, d), lambda b, i: (0, 0)),
            pl.BlockSpec((dy, d), lambda b, i: (1, 0)),
        ],
        out_specs=x_spec,
        out_shape=jax.ShapeDtypeStruct(x.shape, F32),
        compiler_params=_params("arbitrary", "arbitrary"),
        name="mix0_out",
    )(x, attn, y, gt, w_out, w_out)


SCONV_TC = 512


def _sconv_kernel(x_ref, g_ref, sh_ref, sc_ref, gt_ref, prev_ref, cw_ref, win_ref, wout_ref,
                  o_ref, buf_ref, h_scr, rs_scr, vpad, carry, *, tm):
    @pl.when(pl.program_id(1) == 0)
    def _():
        carry[...] = prev_ref[...]

    _modnorm_store(h_scr, x_ref, g_ref, sc_ref, sh_ref, rs_scr)
    h = h_scr[...]
    d = x_ref.shape[1]
    k = cw_ref.shape[0]
    tc = SCONV_TC
    acc = None
    for c in range(d // tc):
        cols = slice(c * tc, (c + 1) * tc)
        r3 = jnp.dot(h, win_ref[:, 3 * c * tc:3 * (c + 1) * tc], preferred_element_type=F32)
        gate_b, v = r3[:, :tc], r3[:, tc:2 * tc] * r3[:, 2 * tc:]
        vpad[pl.ds(0, SUBLANES), cols] = carry[:, cols]
        vpad[pl.ds(SUBLANES, tm), cols] = v
        u = cw_ref[k - 1:k, cols] * v
        for t in range(k - 1):
            u = u + cw_ref[t:t + 1, cols] * vpad[pl.ds(SUBLANES - (k - 1) + t, tm), cols]
        carry[:, cols] = vpad[pl.ds(tm, SUBLANES), cols]
        part = jnp.dot((gate_b * u).astype(BF16), wout_ref[cols, :], preferred_element_type=F32)
        acc = part if acc is None else acc + part
    o_ref[...] = x_ref[...] + gt_ref[...] * acc
    buf_ref[...] = carry[...]


def sconv_mixer(x, g, sh, sc, gt, prev, conv_w, w_in, w_out, *, tm):
    b, s, d = x.shape
    x_spec = pl.BlockSpec((None, tm, d), lambda b, i: (b, i, 0))
    buf_spec = pl.BlockSpec((None, SUBLANES, d), lambda b, i: (b, 0, 0))
    const = lambda shape: pl.BlockSpec(shape, lambda b, i: (0, 0))
    resident = lambda shape: pl.BlockSpec(shape, lambda b, i: (0, 0), pipeline_mode=pl.Buffered(1))
    return pl.pallas_call(
        functools.partial(_sconv_kernel, tm=tm),
        grid=(b, s // tm),
        in_specs=[
            x_spec, const((1, d)), _mod_spec(1, d), _mod_spec(1, d), _mod_spec(1, d), buf_spec,
            const((SCONV_WIDTH, d)), resident((d, 3 * d)), resident((d, d)),
        ],
        out_specs=[x_spec, buf_spec],
        out_shape=[jax.ShapeDtypeStruct(x.shape, F32), jax.ShapeDtypeStruct((b, SUBLANES, d), F32)],
        scratch_shapes=[
            pltpu.VMEM((tm, d), BF16), pltpu.VMEM((tm, LANES), F32),
            pltpu.VMEM((tm + SUBLANES, d), F32), pltpu.VMEM((SUBLANES, d), F32),
        ],
        compiler_params=_params("arbitrary", "arbitrary"),
        name="sconv_mixer",
    )(x, g.reshape(1, d), sh, sc, gt, prev, conv_w.astype(F32), w_in, w_out)


def _arrange_w_in1(w):
    d = w.shape[0]
    w = w.reshape(d, 3, d // SCONV_TC, SCONV_TC)
    return jnp.transpose(w, (0, 2, 1, 3)).reshape(d, 3 * d).astype(BF16)


def _arrange_w_in0(w):
    i1 = ATTN_Q_DIM
    i2 = i1 + ATTN_KV_DIM
    i3 = i2 + ATTN_KV_DIM
    i4 = i3 + SSD_INNER
    i5 = i4 + SSD_CONV_DIM
    q, k, v, z, xbc, dt = (w[:, a:b] for a, b in ((0, i1), (i1, i2), (i2, i3), (i3, i4), (i4, i5), (i5, w.shape[1])))
    pad = jnp.zeros((w.shape[0], PB_WIDTH - PB_DT - dt.shape[1]), w.dtype)
    return jnp.concatenate([q, z, xbc, k, v, dt, pad], axis=1).astype(BF16)


def _split_mod(mod):
    return [m[:, None, :] for m in jnp.split(mod, N_MOD, axis=-1)]


def _per_token(m, length):
    b, _, d = m.shape
    return jnp.broadcast_to(m, (b, length, d)).reshape(1, b * length, d)


def _trunk(x, mods, weights, cache, *, tm, tm_proj, tm_mix, q_ssd, qb_attn):
    b, s, d = x.shape
    flat = cache is not None
    if flat:
        as_rows = lambda t: t.reshape(1, b * s, t.shape[-1])
        mod_of = lambda m: _per_token(m, s)
    else:
        as_rows = lambda t: t
        mod_of = lambda m: m
    unrow = lambda t: t.reshape(b, s, t.shape[-1])
    w = weights

    def ffn(xr, l, k, sh, sc, gt, final_g=None):
        return ffn_half(xr, w["norm_g"][l, 2 * k], mod_of(sh), mod_of(sc), mod_of(gt),
                        w["ffn_gate"], w["ffn_up"], w["ffn_down"], l, k, final_g, tm=tm)

    sh1, sc1, g1, sh2, sc2, g2, sh3, sc3, g3 = mods[0]
    xr = ffn(as_rows(x), 0, 0, sh1, sc1, g1)
    pa, pb = modnorm_proj(xr, w["norm_g"][0, 1], mod_of(sh2), mod_of(sc2), w["w_in0"], tm=tm_proj,
                          tail_width=PB_WIDTH)
    pa, pb = unrow(pa), unrow(pb)
    if cache is None:
        attn = attn_prompt(pa, pb, w["sinks"], qb=qb_attn)
        x_prev = jnp.zeros((b, SUBLANES, SSD_INNER), F32)
        bc_prev = jnp.zeros((b, SUBLANES, SSD_BC_DIM), F32)
        h0 = jnp.zeros((b, SSD_HEADS // 2, SSD_STATE, LANES), F32)
        y, h_new = ssd_mixer(pa, pb, x_prev, bc_prev, h0, *w["ssd"], q=q_ssd)
        new_kv = pb[:, s - WINDOW:]
        sconv_prev = jnp.zeros((b, SUBLANES, d), F32)
    else:
        k_cache, v_cache, h_state, conv_prev, sconv_state = cache
        rows = k_cache.shape[1]
        attn = attn_sample(pa, pb, k_cache.reshape(b, rows, ATTN_KV_DIM), v_cache.reshape(b, rows, ATTN_KV_DIM),
                           w["sinks"])
        tail = _tail_rows(conv_prev)
        pad_rows = lambda t: jnp.pad(t, ((0, 0), (0, q_ssd - s), (0, 0)))
        y, h_new = ssd_mixer(pad_rows(pa), pad_rows(pb), tail[:, :, :SSD_INNER], tail[:, :, SSD_INNER:],
                             _pair_state(h_state), *w["ssd"], q=q_ssd, valid_len=s)
        y = y[:, :s]
        new_kv = pb
        sconv_prev = _tail_rows(sconv_state)
    xbc_raw = pa[:, s - (SSD_CONV - 1):, PA_X:PA_X + SSD_CONV_DIM].astype(F32)
    kv_shape = (b, new_kv.shape[1], ATTN_KV_HEADS, HEAD_DIM)
    states0 = (new_kv[:, :, PB_K:PB_K + ATTN_KV_DIM].reshape(kv_shape),
               new_kv[:, :, PB_V:PB_V + ATTN_KV_DIM].reshape(kv_shape), _unpair_state(h_new), xbc_raw)
    xr = mix0_out(xr, as_rows(attn), as_rows(y), mod_of(g2), w["w_out0"], tm=tm_mix)
    xr = ffn(xr, 0, 1, sh3, sc3, g3)

    sh1, sc1, g1, sh2, sc2, g2, sh3, sc3, g3 = mods[1]
    xr = ffn(xr, 1, 0, sh1, sc1, g1)
    xm, buf = sconv_mixer(unrow(xr), w["norm_g"][1, 1], sh2, sc2, g2, sconv_prev, w["sconv_w"], w["w_in1"],
                          w["w_out1"], tm=min(tm_mix, s))
    sconv_new = buf[:, SUBLANES - (SCONV_WIDTH - 1):]
    y_out = unrow(ffn(as_rows(xm), 1, 1, sh3, sc3, g3, final_g=w["final_g"]))
    return y_out, states0, sconv_new


def kernel(x_prompt, x_sample, c_prompt, c_sample, cache_swa_k, cache_swa_v, state_ssd, state_ssd_conv, state_sconv, norm_g, w_ada, b_ada, w_ffn_gate, w_ffn_up, w_ffn_down, w_in_mix0, w_out_mix0, attn_sinks, ssd_conv_w, ssd_conv_b, ssd_dt_bias, ssd_a_log, ssd_d, ssd_norm_g, w_in_mix1, sconv_w, w_out_mix1, final_norm_g):
    bp, bs = c_prompt.shape[0], c_sample.shape[0]
    c_all = jnp.concatenate([c_prompt, c_sample], axis=0)
    c_rows = -(-c_all.shape[0] // SUBLANES) * SUBLANES
    c_all = jnp.pad(c_all, ((0, c_rows - c_all.shape[0]), (0, 0)))
    mod = ada_modulation(c_all, w_ada, b_ada)
    mods_p = [_split_mod(mod[l, :bp]) for l in range(mod.shape[0])]
    mods_s = [_split_mod(mod[l, bp:bp + bs]) for l in range(mod.shape[0])]

    weights = {
        "norm_g": norm_g,
        "ffn_gate": w_ffn_gate.astype(BF16), "ffn_up": w_ffn_up.astype(BF16), "ffn_down": w_ffn_down.astype(BF16),
        "w_in0": _arrange_w_in0(w_in_mix0[0]), "w_out0": w_out_mix0[0].astype(BF16),
        "sinks": attn_sinks[0],
        "ssd": (ssd_conv_w[0].astype(F32), ssd_conv_b[0].astype(F32), ssd_dt_bias[0], ssd_a_log[0], ssd_d[0],
                ssd_norm_g[0]),
        "w_in1": _arrange_w_in1(w_in_mix1[0]), "sconv_w": sconv_w[0], "w_out1": w_out_mix1[0].astype(BF16),
        "final_g": final_norm_g,
    }
    seq = x_prompt.shape[1]
    y_p, (k_p, v_p, h_p, cv_p), sc_p = _trunk(
        x_prompt, mods_p, weights, None, tm=min(512, seq), tm_proj=min(1024, seq), tm_mix=min(256, seq),
        q_ssd=min(256, seq), qb_attn=min(256, seq))
    n_s = x_sample.shape[0] * x_sample.shape[1]
    cache = (cache_swa_k[0], cache_swa_v[0], state_ssd[0], state_ssd_conv[0], state_sconv[0])
    y_s, (k_s, v_s, h_s, cv_s), sc_s = _trunk(
        x_sample, mods_s, weights, cache, tm=n_s, tm_proj=n_s, tm_mix=n_s, q_ssd=LANES, qb_attn=None)
    stack = lambda t: t[None]
    return (y_p, y_s, stack(k_p), stack(v_p), stack(h_p), stack(cv_p), stack(sc_p),
            stack(k_s), stack(v_s), stack(h_s), stack(cv_s), stack(sc_s))
```

```python
import functools
from typing import NamedTuple

import numpy as np
import jax
import jax.numpy as jnp
from jax import lax
from jax.experimental import pallas as pl
from jax.experimental.pallas import tpu as pltpu

F32 = jnp.float32
BF16 = jnp.bfloat16

EPS = 1e-6
CHUNK = 64
PAST_LEN = 2048
ATTN_HEADS = 16
ATTN_KV_HEADS = 2
ATTN_GROUP = ATTN_HEADS // ATTN_KV_HEADS
HEAD_DIM = 64
WINDOW = 128
WINDOW_CHUNKS = WINDOW // CHUNK
ATTN_Q_DIM = ATTN_HEADS * HEAD_DIM
ATTN_KV_DIM = ATTN_KV_HEADS * HEAD_DIM
SSD_HEADS = 16
SSD_HEAD_DIM = 64
SSD_INNER = SSD_HEADS * SSD_HEAD_DIM
SSD_STATE = 128
SSD_GROUPS = 2
SSD_HPG = SSD_HEADS // SSD_GROUPS
SSD_CONV = 4
SSD_BC_DIM = 2 * SSD_GROUPS * SSD_STATE
SSD_CONV_DIM = SSD_INNER + SSD_BC_DIM
SCONV_WIDTH = 3
N_MOD = 9

LANES = 128
SUBLANES = 8
MXU_DIM = 256
VMEM_LIMIT = 56 * 1024 * 1024

PROJ_TN = 512
PA_Q = 0
PA_Z = PA_Q + ATTN_Q_DIM
PA_X = PA_Z + SSD_INNER
PA_BC = PA_X + SSD_INNER
PA_WIDTH = PA_BC + SSD_BC_DIM
PB_K = 0
PB_V = PB_K + ATTN_KV_DIM
PB_DT = PB_V + ATTN_KV_DIM
PB_WIDTH = PROJ_TN
ATTN_KEY_PAD = MXU_DIM
assert HEAD_DIM * 2 == LANES and ATTN_KV_DIM == LANES and PA_WIDTH % PROJ_TN == 0


def _params(*sem):
    return pltpu.CompilerParams(dimension_semantics=sem, vmem_limit_bytes=VMEM_LIMIT)


def _silu(x):
    return x / (1.0 + jnp.exp(-x))


def _softplus(x):
    return jnp.maximum(x, 0.0) + jnp.log(1.0 + jnp.exp(-jnp.abs(x)))


NORM_ROWS = 2 * SUBLANES


def _modnorm_store(h_ref, x_ref, g_ref, sc_ref, sh_ref, rs_scr):
    tm, d = x_ref.shape
    for c in range(tm // SUBLANES):
        r = pl.ds(c * SUBLANES, SUBLANES)
        x = x_ref[r, :]
        ms = jnp.mean(x * x, axis=-1, keepdims=True)
        rs_scr[r, :] = jnp.broadcast_to(lax.rsqrt(ms + EPS), (SUBLANES, LANES))
    per_row = sc_ref.shape[0] != 1
    scale = None if per_row else g_ref[...] * (1.0 + sc_ref[...])

    def body(c, carry):
        r = pl.ds(pl.multiple_of(c * NORM_ROWS, NORM_ROWS), NORM_ROWS)
        rs = jnp.concatenate([rs_scr[r, :]] * (d // LANES), axis=1)
        y = x_ref[r, :] * rs
        if per_row:
            h = y * (g_ref[...] * (1.0 + sc_ref[r, :])) + sh_ref[r, :]
        else:
            h = y * scale + sh_ref[...]
        h_ref[r, :] = h.astype(BF16)
        return carry

    lax.fori_loop(0, tm // NORM_ROWS, body, 0, unroll=2 if tm >= 2 * NORM_ROWS else 1)


def _mod_spec(mod_rows, d):
    if mod_rows == 1:
        return pl.BlockSpec((None, 1, d), lambda b, i, *_: (b, 0, 0))
    return pl.BlockSpec((None, mod_rows, d), lambda b, i, *_: (b, i, 0))


def _ada_kernel(c_ref, w_ref, b_ref, o_ref):
    c = c_ref[...]
    s = _silu(c).astype(BF16)
    o_ref[...] = jnp.dot(s, w_ref[...].astype(BF16), preferred_element_type=F32) + b_ref[...]


def ada_modulation(c, w_ada, b_ada, tn=1024):
    n_layers, d, n = w_ada.shape
    r = c.shape[0]
    return pl.pallas_call(
        _ada_kernel,
        grid=(n_layers, n // tn),
        in_specs=[
            pl.BlockSpec((r, d), lambda l, j: (0, 0)),
            pl.BlockSpec((None, d, tn), lambda l, j: (l, 0, j)),
            pl.BlockSpec((None, 1, tn), lambda l, j: (l, 0, j)),
        ],
        out_specs=pl.BlockSpec((None, r, tn), lambda l, j: (l, 0, j)),
        out_shape=jax.ShapeDtypeStruct((n_layers, r, n), F32),
        compiler_params=_params("arbitrary", "arbitrary"),
        name="ada_modulation",
    )(c, w_ada, b_ada.reshape(n_layers, 1, n))


def _residual_store(o_ref, x_ref, gt_ref, rs_scr, fg_ref):
    tm, d = x_ref.shape
    per_row = gt_ref.shape[0] != 1
    n_steps = tm // NORM_ROWS
    unroll = 2 if n_steps >= 2 else 1

    def residual(c, carry):
        r = pl.ds(pl.multiple_of(c * NORM_ROWS, NORM_ROWS), NORM_ROWS)
        gt = gt_ref[r, :] if per_row else gt_ref[...]
        o_ref[r, :] = x_ref[r, :] + 0.5 * gt * o_ref[r, :]
        return carry

    lax.fori_loop(0, n_steps, residual, 0, unroll=unroll)
    if fg_ref is None:
        return
    for c in range(tm // SUBLANES):
        r = pl.ds(c * SUBLANES, SUBLANES)
        xn = o_ref[r, :]
        ms = jnp.mean(xn * xn, axis=-1, keepdims=True)
        rs_scr[r, :] = jnp.broadcast_to(lax.rsqrt(ms + EPS), (SUBLANES, LANES))

    def scale(c, carry):
        r = pl.ds(pl.multiple_of(c * NORM_ROWS, NORM_ROWS), NORM_ROWS)
        rs = jnp.concatenate([rs_scr[r, :]] * (d // LANES), axis=1)
        o_ref[r, :] = o_ref[r, :] * rs * fg_ref[...]
        return carry

    lax.fori_loop(0, n_steps, scale, 0, unroll=unroll)


def _ffn_kernel(x_ref, g_ref, sh_ref, sc_ref, gt_ref, wg_ref, wu_ref, wd_ref, *rest, n_ff, final):
    if final:
        fg_ref, o_ref, h_scr, rs_scr = rest
    else:
        fg_ref = None
        o_ref, h_scr, rs_scr = rest
    j = pl.program_id(2)

    @pl.when(j == 0)
    def _():
        _modnorm_store(h_scr, x_ref, g_ref, sc_ref, sh_ref, rs_scr)

    h = h_scr[...]
    half = wg_ref.shape[1] // 2
    acts = []
    for c in range(2):
        cols = slice(c * half, (c + 1) * half)
        g = jnp.dot(h, wg_ref[:, cols], preferred_element_type=F32)
        u = jnp.dot(h, wu_ref[:, cols], preferred_element_type=F32)
        acts.append((_silu(g) * u).astype(BF16))
    total = jnp.where(j == 0, 0.0, o_ref[...])
    for c in range(2):
        total = total + jnp.dot(acts[c], wd_ref[c * half:(c + 1) * half, :], preferred_element_type=F32)
    o_ref[...] = total

    @pl.when(j == n_ff - 1)
    def _():
        _residual_store(o_ref, x_ref, gt_ref, rs_scr, fg_ref)


def ffn_half(x, g, sh, sc, gt, wg, wu, wd, layer, half, final_g=None, *, tm, tf=512):
    b, s, d = x.shape
    f = wg.shape[-1]
    n_ff = f // tf
    mod_rows = 1 if sh.shape[1] == 1 else tm
    final = final_g is not None
    x_spec = pl.BlockSpec((None, tm, d), lambda b, i, j: (b, i, 0))
    in_specs = [
        x_spec,
        pl.BlockSpec((1, d), lambda b, i, j: (0, 0)),
        _mod_spec(mod_rows, d), _mod_spec(mod_rows, d), _mod_spec(mod_rows, d),
        pl.BlockSpec((None, None, d, tf), lambda b, i, j: (layer, half, 0, j)),
        pl.BlockSpec((None, None, d, tf), lambda b, i, j: (layer, half, 0, j)),
        pl.BlockSpec((None, None, tf, d), lambda b, i, j: (layer, half, j, 0)),
    ]
    args = [x, g.reshape(1, d), sh, sc, gt, wg, wu, wd]
    if final:
        in_specs.append(pl.BlockSpec((1, d), lambda b, i, j: (0, 0)))
        args.append(final_g.reshape(1, d))
    return pl.pallas_call(
        functools.partial(_ffn_kernel, n_ff=n_ff, final=final),
        grid=(b, s // tm, n_ff),
        in_specs=in_specs,
        out_specs=x_spec,
        out_shape=jax.ShapeDtypeStruct(x.shape, F32),
        scratch_shapes=[pltpu.VMEM((tm, d), BF16), pltpu.VMEM((tm, LANES), F32)],
        compiler_params=_params("arbitrary", "arbitrary", "arbitrary"),
        name="ffn_half_final" if final else "ffn_half",
    )(*args)


def _proj_kernel(x_ref, g_ref, sh_ref, sc_ref, w_ref, *rest, n_main):
    j = pl.program_id(2)
    h_scr, rs_scr = rest[-2:]

    @pl.when(j == 0)
    def _():
        _modnorm_store(h_scr, x_ref, g_ref, sc_ref, sh_ref, rs_scr)

    r = jnp.dot(h_scr[...], w_ref[...], preferred_element_type=F32)
    if n_main is None:
        rest[0][...] = r.astype(rest[0].dtype)
    else:
        main_ref, tail_ref = rest[0], rest[1]

        @pl.when(j < n_main)
        def _():
            main_ref[...] = r.astype(main_ref.dtype)

        @pl.when(j >= n_main)
        def _():
            tail_ref[...] = r


def modnorm_proj(x, g, sh, sc, w, *, tm, tn=PROJ_TN, tail_width=None):
    b, s, d = x.shape
    n = w.shape[1]
    mod_rows = 1 if sh.shape[1] == 1 else tm
    if tail_width is None:
        n_main = None
        out_specs = pl.BlockSpec((None, tm, tn), lambda b, i, j: (b, i, j))
        out_shape = jax.ShapeDtypeStruct((b, s, n), BF16)
    else:
        assert tail_width == tn
        n_main = (n - tail_width) // tn
        out_specs = [
            pl.BlockSpec((None, tm, tn), lambda b, i, j: (b, i, jnp.minimum(j, n_main - 1))),
            pl.BlockSpec((None, tm, tn), lambda b, i, j: (b, i, 0)),
        ]
        out_shape = [jax.ShapeDtypeStruct((b, s, n - tail_width), BF16),
                     jax.ShapeDtypeStruct((b, s, tail_width), F32)]
    return pl.pallas_call(
        functools.partial(_proj_kernel, n_main=n_main),
        grid=(b, s // tm, n // tn),
        in_specs=[
            pl.BlockSpec((None, tm, d), lambda b, i, j: (b, i, 0)),
            pl.BlockSpec((1, d), lambda b, i, j: (0, 0)),
            _mod_spec(mod_rows, d), _mod_spec(mod_rows, d),
            pl.BlockSpec((d, tn), lambda b, i, j: (0, j)),
        ],
        out_specs=out_specs,
        out_shape=out_shape,
        scratch_shapes=[pltpu.VMEM((tm, d), BF16), pltpu.VMEM((tm, LANES), F32)],
        compiler_params=_params("arbitrary", "arbitrary", "arbitrary"),
        name="modnorm_proj",
    )(x, g.reshape(1, d), sh, sc, w)


def _kv_variants(t):
    low = lax.broadcasted_iota(jnp.int32, (1, LANES), 1) < HEAD_DIM
    swapped = pltpu.roll(t, HEAD_DIM, 1)
    zero = jnp.zeros_like(t)
    return {
        (0, 0): jnp.where(low, t, zero).astype(BF16), (0, 1): jnp.where(low, zero, swapped).astype(BF16),
        (1, 0): jnp.where(low, swapped, zero).astype(BF16), (1, 1): jnp.where(low, zero, t).astype(BF16),
    }


def _pad_rows(x, rows):
    return jnp.concatenate([x, jnp.zeros((rows - x.shape[0], x.shape[1]), x.dtype)], axis=0)


def _attn_rows(q_rows, k_var, v_var, bias_of):
    r = q_rows.shape[0]
    blocks_per_kv = ATTN_GROUP // 2
    low = lax.broadcasted_iota(jnp.int32, (ATTN_KEY_PAD, LANES), 1) < HEAD_DIM
    ones = (jnp.where(low, 1.0, 0.0).astype(BF16), jnp.where(low, 0.0, 1.0).astype(BF16))
    outs = []
    for h in range(ATTN_KV_HEADS):
        stack = jnp.concatenate(
            [q_rows[:, (h * blocks_per_kv + t) * LANES:(h * blocks_per_kv + t + 1) * LANES]
             for t in range(blocks_per_kv)], axis=0)
        stack = (stack.astype(F32) * (HEAD_DIM ** -0.5)).astype(BF16)
        res = None
        for half in range(2):
            keys = _pad_rows(k_var[h, half], ATTN_KEY_PAD)
            s = lax.dot_general(stack, keys, (((1,), (1,)), ((), ())), preferred_element_type=F32)
            s = s + bias_of(h, half)
            p = jnp.exp(s - jnp.max(s, axis=1, keepdims=True)).astype(BF16)
            rhs = jnp.concatenate([_pad_rows(v_var[h, half], ATTN_KEY_PAD), ones[half]], axis=1)
            part = jnp.dot(p, rhs, preferred_element_type=F32)
            res = part if res is None else res + part
        blk = res[:, :LANES] / res[:, LANES:]
        outs.extend(blk[t * r:(t + 1) * r] for t in range(blocks_per_kv))
    return jnp.concatenate(outs, axis=1)


def _alibi_slopes():
    return 2.0 ** (-8.0 * np.arange(1, ATTN_HEADS + 1) / ATTN_HEADS)


def _attn_bias(dist, valid, sinks):
    q, s = dist.shape
    blocks = ATTN_GROUP // 2
    order = lambda a: np.transpose(a.reshape((ATTN_KV_HEADS, blocks, 2) + a.shape[1:]), (0, 2, 1, 3, 4))
    base = np.where(valid[None], -_alibi_slopes()[:, None, None] * dist[None], -np.inf)
    base = order(base).reshape(ATTN_KV_HEADS, 2, blocks * q, s).astype(np.float32)
    sink = jnp.transpose(sinks.astype(F32).reshape(ATTN_KV_HEADS, blocks, 2), (0, 2, 1))
    sink = jnp.broadcast_to(sink[:, :, :, None, None], (ATTN_KV_HEADS, 2, blocks, q, 1))
    sink = sink.reshape(ATTN_KV_HEADS, 2, blocks * q, 1)
    tail = jnp.full((ATTN_KV_HEADS, 2, blocks * q, ATTN_KEY_PAD - s - 1), -jnp.inf, F32)
    return jnp.concatenate([jnp.asarray(base), sink, tail], axis=-1)


def _attn_prompt_kernel(q_ref, kvc_ref, kvp_ref, bias_ref, o_ref, *, n_chunks):
    i = pl.program_id(1)
    kv = jnp.concatenate([kvp_ref[...], kvc_ref[...]], axis=0)
    k_var = _kv_variants(kv[:, :ATTN_KV_DIM])
    v_var = _kv_variants(kv[:, ATTN_KV_DIM:])
    span = (WINDOW_CHUNKS + 1) * CHUNK
    for cc in range(n_chunks):
        variant = jnp.minimum(i * n_chunks + cc, WINDOW_CHUNKS)
        band = slice(cc * CHUNK, cc * CHUNK + span)
        rows = pl.ds(cc * CHUNK, CHUNK)
        o_ref[rows, :] = _attn_rows(
            q_ref[rows, :], {key: t[band] for key, t in k_var.items()}, {key: t[band] for key, t in v_var.items()},
            lambda h, half: bias_ref[variant, h, half]).astype(o_ref.dtype)


def attn_prompt(pa, pb, sinks, *, qb, out_dtype=BF16):
    b, s, _ = pa.shape
    n_chunks = qb // CHUNK
    span = (WINDOW_CHUNKS + 1) * CHUNK
    qi = np.arange(CHUNK)
    kj = np.arange(span)
    dist = np.abs(qi[:, None] + WINDOW_CHUNKS * CHUNK - kj[None, :]).astype(np.float64)
    bias = jnp.stack([
        _attn_bias(dist, np.broadcast_to(c - WINDOW_CHUNKS + kj[None, :] // CHUNK >= 0, dist.shape), sinks)
        for c in range(WINDOW_CHUNKS + 1)])
    kv_w = 2 * ATTN_KV_DIM
    return pl.pallas_call(
        functools.partial(_attn_prompt_kernel, n_chunks=n_chunks),
        grid=(b, s // qb),
        in_specs=[
            pl.BlockSpec((None, qb, ATTN_Q_DIM), lambda b, i: (b, i, PA_Q // ATTN_Q_DIM)),
            pl.BlockSpec((None, qb, kv_w), lambda b, i: (b, i, PB_K // kv_w)),
            pl.BlockSpec((None, WINDOW, kv_w), lambda b, i: (b, jnp.maximum(i * (qb // WINDOW) - 1, 0), PB_K // kv_w)),
            pl.BlockSpec(bias.shape, lambda b, i: (0,) * bias.ndim),
        ],
        out_specs=pl.BlockSpec((None, qb, ATTN_Q_DIM), lambda b, i: (b, i, 0)),
        out_shape=jax.ShapeDtypeStruct((b, s, ATTN_Q_DIM), out_dtype),
        compiler_params=_params("arbitrary", "arbitrary"),
        name="attn_prompt",
    )(pa, pb, pb, bias)


def _attn_sample_kernel(q_ref, kvn_ref, kc_ref, vc_ref, bias_ref, o_ref):
    kvn = kvn_ref[...]
    k_var = _kv_variants(jnp.concatenate([kc_ref[...], kvn[:, :ATTN_KV_DIM]], axis=0))
    v_var = _kv_variants(jnp.concatenate([vc_ref[...], kvn[:, ATTN_KV_DIM:]], axis=0))
    o_ref[...] = _attn_rows(q_ref[...], k_var, v_var, lambda h, half: bias_ref[h, half]).astype(o_ref.dtype)


def attn_sample(pa, pb, k_cache, v_cache, sinks, *, out_dtype=BF16):
    b, length, _ = pa.shape
    rows = k_cache.shape[1]
    qpos = PAST_LEN + np.arange(length)
    kpos = PAST_LEN - rows + np.arange(rows + length)
    qch, kch = qpos // CHUNK, kpos // CHUNK
    valid = (kch[None, :] <= qch[:, None]) & (kch[None, :] >= qch[:, None] - WINDOW_CHUNKS)
    dist = np.abs(qpos[:, None] - kpos[None, :]).astype(np.float64)
    bias = _attn_bias(dist, valid, sinks)
    kv_w = 2 * ATTN_KV_DIM
    cache = pl.BlockSpec((None, rows, ATTN_KV_DIM), lambda b: (b, 0, 0))
    return pl.pallas_call(
        _attn_sample_kernel,
        grid=(b,),
        in_specs=[
            pl.BlockSpec((None, length, ATTN_Q_DIM), lambda b: (b, 0, PA_Q // ATTN_Q_DIM)),
            pl.BlockSpec((None, length, kv_w), lambda b: (b, 0, PB_K // kv_w)),
            cache, cache,
            pl.BlockSpec(bias.shape, lambda b: (0,) * bias.ndim),
        ],
        out_specs=pl.BlockSpec((None, length, ATTN_Q_DIM), lambda b: (b, 0, 0)),
        out_shape=jax.ShapeDtypeStruct((b, length, ATTN_Q_DIM), out_dtype),
        compiler_params=_params("arbitrary"),
        name="attn_sample",
    )(pa, pb, k_cache, v_cache, bias)


def _causal_conv(pad_ref, prev_ref, cur, w_ref, b_ref, first, q):
    k = w_ref.shape[0]

    @pl.when(first)
    def _():
        pad_ref[pl.ds(0, SUBLANES), :] = prev_ref[...]

    pad_ref[pl.ds(SUBLANES, q), :] = cur
    acc = b_ref[...] + w_ref[k - 1:k, :] * cur
    for t in range(k - 1):
        acc = acc + w_ref[t:t + 1, :] * pad_ref[pl.ds(SUBLANES - (k - 1) + t, q), :]
    pad_ref[pl.ds(0, SUBLANES), :] = pad_ref[pl.ds(q, SUBLANES), :]
    return acc


def _split3(x):
    hi = x.astype(BF16)
    r = x - hi.astype(F32)
    mid = r.astype(BF16)
    lo = (r - mid.astype(F32)).astype(BF16)
    return hi, mid, lo


def _ssd_kernel(x_ref, bc_ref, z_ref, dt_ref, xprev_ref, bcprev_ref, h0_ref,
                wx_ref, bx_ref, wbc_ref, bbc_ref, dtb_ref, alog_ref, dskip_ref, ng_ref,
                y_ref, hout_ref, xpad, bcpad, h_scr, *, q, n_blocks, valid_len):
    i = pl.program_id(1)
    first = i == 0

    @pl.when(first)
    def _():
        h_scr[...] = h0_ref[...]

    xs = _silu(_causal_conv(xpad, xprev_ref, x_ref[...].astype(F32), wx_ref, bx_ref, first, q))
    bc = _silu(_causal_conv(bcpad, bcprev_ref, bc_ref[...].astype(F32), wbc_ref, bbc_ref, first, q))
    xs_b = xs.astype(BF16)

    dt = _softplus(dt_ref[...] + dtb_ref[...])
    if valid_len < q:
        row = lax.broadcasted_iota(jnp.int32, dt.shape, 0)
        dt = jnp.where(row < valid_len, dt, 0.0)
    da = dt * (-jnp.exp(alog_ref[...]))
    r_io = lax.broadcasted_iota(jnp.int32, (q, q), 0)
    c_io = lax.broadcasted_iota(jnp.int32, (q, q), 1)
    causal = c_io <= r_io
    tri = jnp.where(causal, 1.0, 0.0).astype(BF16)
    cum = sum(jnp.dot(tri, part, preferred_element_type=F32) for part in _split3(da))
    cum_t = cum.T
    dt_t = dt.T
    lane = lax.broadcasted_iota(jnp.int32, (1, LANES), 1)
    low_half = lane < SSD_HEAD_DIM

    y_pairs = []
    for g in range(SSD_GROUPS):
        b_g = bc[:, g * SSD_STATE:(g + 1) * SSD_STATE]
        c_g = bc[:, (SSD_GROUPS + g) * SSD_STATE:(SSD_GROUPS + g + 1) * SSD_STATE]
        cb = lax.dot_general(c_g.astype(BF16), b_g.astype(BF16), (((1,), (1,)), ((), ())),
                             preferred_element_type=F32)
        b_t = b_g.T
        for pr in range(g * SSD_HPG // 2, (g + 1) * SSD_HPG // 2):
            x_pair = xs_b[:, pr * LANES:(pr + 1) * LANES]
            h_pair = h_scr[pr]
            h_pair_b = h_pair.astype(BF16)
            ys, states, decays = [], [], []
            for j in (2 * pr, 2 * pr + 1):
                col = jnp.broadcast_to(cum[:, j:j + 1], (q, q))
                row = cum_t[j:j + 1, :]
                decay_in = jnp.exp(jnp.where(causal, col - row, -jnp.inf))
                m = (cb * decay_in * dt_t[j:j + 1, :]).astype(BF16)
                c_scaled = (c_g * jnp.exp(col[:, :SSD_STATE])).astype(BF16)
                ys.append(jnp.dot(m, x_pair, preferred_element_type=F32)
                          + jnp.dot(c_scaled, h_pair_b, preferred_element_type=F32))
                last = row[:, q - 1:q]
                w_row = dt_t[j:j + 1, :] * jnp.exp(last - row)
                states.append(jnp.dot((b_t * w_row).astype(BF16), x_pair, preferred_element_type=F32))
                decays.append(jnp.exp(last))
            h_scr[pr] = (jnp.where(low_half, decays[0], decays[1]) * h_pair
                         + jnp.where(low_half, states[0], states[1]))
            y_pairs.append(jnp.where(low_half, ys[0], ys[1]))
    y = jnp.concatenate(y_pairs, axis=1) + dskip_ref[...] * xs
    y = y * _silu(z_ref[...].astype(F32))
    half = SSD_INNER // SSD_GROUPS
    normed = []
    for g in range(SSD_GROUPS):
        yg = y[:, g * half:(g + 1) * half]
        normed.append(yg * lax.rsqrt(jnp.mean(yg * yg, axis=-1, keepdims=True) + EPS))
    y_ref[...] = (jnp.concatenate(normed, axis=1) * ng_ref[...]).astype(y_ref.dtype)

    @pl.when(i == n_blocks - 1)
    def _():
        hout_ref[...] = h_scr[...]


def ssd_mixer(pa, pb, x_prev, bc_prev, h0, conv_w, conv_b, dt_bias, a_log, d_skip, norm_g, *, q, valid_len=None,
              out_dtype=BF16):
    b, s, _ = pa.shape
    n_blocks = s // q
    valid_len = q if valid_len is None else valid_len
    n_pairs = SSD_HEADS // 2
    pad128 = lambda v: jnp.pad(v.astype(F32), (0, LANES - v.shape[0])).reshape(1, LANES)
    col = lambda width, off: pl.BlockSpec((None, q, width), lambda b, i: (b, i, off // width))
    const = lambda shape: pl.BlockSpec(shape, lambda b, i: (0,) * len(shape))
    per_b = lambda shape: pl.BlockSpec((None,) + shape, lambda b, i: (b,) + (0,) * len(shape))
    h_shape = (n_pairs, SSD_STATE, LANES)
    return pl.pallas_call(
        functools.partial(_ssd_kernel, q=q, n_blocks=n_blocks, valid_len=valid_len),
        grid=(b, n_blocks),
        in_specs=[
            col(SSD_INNER, PA_X), col(SSD_BC_DIM, PA_BC), col(SSD_INNER, PA_Z), col(LANES, PB_DT),
            per_b((SUBLANES, SSD_INNER)), per_b((SUBLANES, SSD_BC_DIM)), per_b(h_shape),
            const((SSD_CONV, SSD_INNER)), const((1, SSD_INNER)),
            const((SSD_CONV, SSD_BC_DIM)), const((1, SSD_BC_DIM)),
            const((1, LANES)), const((1, LANES)), const((1, SSD_INNER)), const((1, SSD_INNER)),
        ],
        out_specs=[
            pl.BlockSpec((None, q, SSD_INNER), lambda b, i: (b, i, 0)),
            per_b(h_shape),
        ],
        out_shape=[
            jax.ShapeDtypeStruct((b, s, SSD_INNER), out_dtype),
            jax.ShapeDtypeStruct((b,) + h_shape, F32),
        ],
        scratch_shapes=[
            pltpu.VMEM((q + SUBLANES, SSD_INNER), F32),
            pltpu.VMEM((q + SUBLANES, SSD_BC_DIM), F32),
            pltpu.VMEM(h_shape, F32),
        ],
        compiler_params=_params("arbitrary", "arbitrary"),
        name="ssd_mixer",
    )(pa, pa, pa, pb, x_prev, bc_prev, h0,
      conv_w[:, :SSD_INNER], conv_b[:SSD_INNER].reshape(1, -1),
      conv_w[:, SSD_INNER:], conv_b[SSD_INNER:].reshape(1, -1),
      pad128(dt_bias), pad128(a_log),
      jnp.repeat(d_skip.astype(F32), SSD_HEAD_DIM).reshape(1, -1), norm_g.astype(F32).reshape(1, -1))


def _pair_state(h):
    b = h.shape[0]
    h = h.astype(F32).reshape(b, SSD_HEADS // 2, 2, SSD_HEAD_DIM, SSD_STATE)
    return h.transpose(0, 1, 4, 2, 3).reshape(b, SSD_HEADS // 2, SSD_STATE, 2 * SSD_HEAD_DIM)


def _unpair_state(h):
    b = h.shape[0]
    h = h.reshape(b, SSD_HEADS // 2, SSD_STATE, 2, SSD_HEAD_DIM)
    return h.transpose(0, 1, 3, 4, 2).reshape(b, SSD_HEADS, SSD_HEAD_DIM, SSD_STATE)


def _tail_rows(prev):
    return jnp.pad(prev.astype(F32), ((0, 0), (SUBLANES - prev.shape[1], 0), (0, 0)))


def _outproj_kernel(x_ref, a_ref, y_ref, gt_ref, wa_ref, wy_ref, o_ref):
    m = (jnp.dot(a_ref[...].astype(BF16), wa_ref[...], preferred_element_type=F32)
         + jnp.dot(y_ref[...].astype(BF16), wy_ref[...], preferred_element_type=F32))
    o_ref[...] = x_ref[...] + gt_ref[...] * m


def mix0_out(x, attn, y, gt, w_out, *, tm):
    b, s, d = x.shape
    da, dy = attn.shape[2], y.shape[2]
    mod_rows = 1 if gt.shape[1] == 1 else tm
    x_spec = pl.BlockSpec((None, tm, d), lambda b, i: (b, i, 0))
    return pl.pallas_call(
        _outproj_kernel,
        grid=(b, s // tm),
        in_specs=[
            x_spec,
            pl.BlockSpec((None, tm, da), lambda b, i: (b, i, 0)),
            pl.BlockSpec((None, tm, dy), lambda b, i: (b, i, 0)),
            _mod_spec(mod_rows, d),
            pl.BlockSpec((da, d), lambda b, i: (0, 0)),
            pl.BlockSpec((dy, d), lambda b, i: (1, 0)),
        ],
        out_specs=x_spec,
        out_shape=jax.ShapeDtypeStruct(x.shape, F32),
        compiler_params=_params("arbitrary", "arbitrary"),
        name="mix0_out",
    )(x, attn, y, gt, w_out, w_out)


SCONV_TC = 512


def _sconv_kernel(x_ref, g_ref, sh_ref, sc_ref, gt_ref, prev_ref, cw_ref, win_ref, wout_ref,
                  o_ref, buf_ref, h_scr, rs_scr, vpad, carry, *, tm):
    @pl.when(pl.program_id(1) == 0)
    def _():
        carry[...] = prev_ref[...]

    _modnorm_store(h_scr, x_ref, g_ref, sc_ref, sh_ref, rs_scr)
    h = h_scr[...]
    d = x_ref.shape[1]
    k = cw_ref.shape[0]
    tc = SCONV_TC
    acc = None
    for c in range(d // tc):
        cols = slice(c * tc, (c + 1) * tc)
        gate_b, gate_c, xi = (jnp.dot(h, win_ref[:, part * d + c * tc:part * d + (c + 1) * tc],
                                      preferred_element_type=F32) for part in range(3))
        v = gate_c * xi
        vpad[pl.ds(0, SUBLANES), cols] = carry[:, cols]
        vpad[pl.ds(SUBLANES, tm), cols] = v
        u = cw_ref[k - 1:k, cols] * v
        for t in range(k - 1):
            u = u + cw_ref[t:t + 1, cols] * vpad[pl.ds(SUBLANES - (k - 1) + t, tm), cols]
        carry[:, cols] = vpad[pl.ds(tm, SUBLANES), cols]
        part = jnp.dot((gate_b * u).astype(BF16), wout_ref[cols, :], preferred_element_type=F32)
        acc = part if acc is None else acc + part
    o_ref[...] = x_ref[...] + gt_ref[...] * acc
    buf_ref[...] = carry[...]


def sconv_mixer(x, g, sh, sc, gt, prev, conv_w, w_in, w_out, *, tm):
    b, s, d = x.shape
    x_spec = pl.BlockSpec((None, tm, d), lambda b, i: (b, i, 0))
    buf_spec = pl.BlockSpec((None, SUBLANES, d), lambda b, i: (b, 0, 0))
    const = lambda shape: pl.BlockSpec(shape, lambda b, i: (0, 0))
    resident = lambda shape: pl.BlockSpec(shape, lambda b, i: (0, 0), pipeline_mode=pl.Buffered(1))
    return pl.pallas_call(
        functools.partial(_sconv_kernel, tm=tm),
        grid=(b, s // tm),
        in_specs=[
            x_spec, const((1, d)), _mod_spec(1, d), _mod_spec(1, d), _mod_spec(1, d), buf_spec,
            const((SCONV_WIDTH, d)), resident((d, 3 * d)), resident((d, d)),
        ],
        out_specs=[x_spec, buf_spec],
        out_shape=[jax.ShapeDtypeStruct(x.shape, F32), jax.ShapeDtypeStruct((b, SUBLANES, d), F32)],
        scratch_shapes=[
            pltpu.VMEM((tm, d), BF16), pltpu.VMEM((tm, LANES), F32),
            pltpu.VMEM((tm + SUBLANES, d), F32), pltpu.VMEM((SUBLANES, d), F32),
        ],
        compiler_params=_params("arbitrary", "arbitrary"),
        name="sconv_mixer",
    )(x, g.reshape(1, d), sh, sc, gt, prev, conv_w.astype(F32), w_in, w_out)


def _arrange_w_in0(w):
    i1 = ATTN_Q_DIM
    i2 = i1 + ATTN_KV_DIM
    i3 = i2 + ATTN_KV_DIM
    i4 = i3 + SSD_INNER
    i5 = i4 + SSD_CONV_DIM
    q, k, v, z, xbc, dt = (w[:, a:b] for a, b in ((0, i1), (i1, i2), (i2, i3), (i3, i4), (i4, i5), (i5, w.shape[1])))
    pad = jnp.zeros((w.shape[0], PB_WIDTH - PB_DT - dt.shape[1]), w.dtype)
    return jnp.concatenate([q, z, xbc, k, v, dt, pad], axis=1).astype(BF16)


def _split_mod(mod):
    return [m[:, None, :] for m in jnp.split(mod, N_MOD, axis=-1)]


def _per_token(m, length):
    b, _, d = m.shape
    return jnp.broadcast_to(m, (b, length, d)).reshape(1, b * length, d)


class _Tiles(NamedTuple):
    ffn: int
    proj: int
    mix_out: int
    sconv: int
    ssd: int
    attn: int


def _tiles(batch, seq, flat):
    if flat:
        rows = batch * seq
        return _Tiles(ffn=rows, proj=rows, mix_out=rows, sconv=seq, ssd=LANES, attn=seq)
    return _Tiles(ffn=min(1024, seq), proj=min(1024, seq), mix_out=min(512, seq), sconv=min(256, seq),
                  ssd=min(256, seq), attn=min(256, seq))


def _trunk(x, mods, weights, cache):
    b, s, d = x.shape
    t = _tiles(b, s, cache is not None)
    tm, tm_proj, tm_mix, q_ssd, qb_attn = t.ffn, t.proj, t.mix_out, t.ssd, t.attn
    flat = cache is not None
    if flat:
        as_rows = lambda t: t.reshape(1, b * s, t.shape[-1])
        mod_of = lambda m: _per_token(m, s)
    else:
        as_rows = lambda t: t
        mod_of = lambda m: m
    unrow = lambda t: t.reshape(b, s, t.shape[-1])
    w = weights

    def ffn(xr, l, k, sh, sc, gt, final_g=None):
        return ffn_half(xr, w["norm_g"][l, 2 * k], mod_of(sh), mod_of(sc), mod_of(gt),
                        w["ffn_gate"], w["ffn_up"], w["ffn_down"], l, k, final_g, tm=tm)

    sh1, sc1, g1, sh2, sc2, g2, sh3, sc3, g3 = mods[0]
    xr = ffn(as_rows(x), 0, 0, sh1, sc1, g1)
    pa, pb = modnorm_proj(xr, w["norm_g"][0, 1], mod_of(sh2), mod_of(sc2), w["w_in0"], tm=tm_proj,
                          tail_width=PB_WIDTH)
    pa, pb = unrow(pa), unrow(pb)
    if cache is None:
        attn = attn_prompt(pa, pb, w["sinks"], qb=qb_attn)
        x_prev = jnp.zeros((b, SUBLANES, SSD_INNER), F32)
        bc_prev = jnp.zeros((b, SUBLANES, SSD_BC_DIM), F32)
        h0 = jnp.zeros((b, SSD_HEADS // 2, SSD_STATE, LANES), F32)
        y, h_new = ssd_mixer(pa, pb, x_prev, bc_prev, h0, *w["ssd"], q=q_ssd)
        new_kv = pb[:, s - WINDOW:]
        sconv_prev = jnp.zeros((b, SUBLANES, d), F32)
    else:
        k_cache, v_cache, h_state, conv_prev, sconv_state = cache
        rows = k_cache.shape[1]
        attn = attn_sample(pa, pb, k_cache.reshape(b, rows, ATTN_KV_DIM), v_cache.reshape(b, rows, ATTN_KV_DIM),
                           w["sinks"])
        tail = _tail_rows(conv_prev)
        pad_rows = lambda t: jnp.pad(t, ((0, 0), (0, q_ssd - s), (0, 0)))
        y, h_new = ssd_mixer(pad_rows(pa), pad_rows(pb), tail[:, :, :SSD_INNER], tail[:, :, SSD_INNER:],
                             _pair_state(h_state), *w["ssd"], q=q_ssd, valid_len=s)
        y = y[:, :s]
        new_kv = pb
        sconv_prev = _tail_rows(sconv_state)
    xbc_raw = pa[:, s - (SSD_CONV - 1):, PA_X:PA_X + SSD_CONV_DIM].astype(F32)
    kv_shape = (b, new_kv.shape[1], ATTN_KV_HEADS, HEAD_DIM)
    states0 = (new_kv[:, :, PB_K:PB_K + ATTN_KV_DIM].reshape(kv_shape),
               new_kv[:, :, PB_V:PB_V + ATTN_KV_DIM].reshape(kv_shape), _unpair_state(h_new), xbc_raw)
    xr = mix0_out(xr, as_rows(attn), as_rows(y), mod_of(g2), w["w_out0"], tm=tm_mix)
    xr = ffn(xr, 0, 1, sh3, sc3, g3)

    sh1, sc1, g1, sh2, sc2, g2, sh3, sc3, g3 = mods[1]
    xr = ffn(xr, 1, 0, sh1, sc1, g1)
    xm, buf = sconv_mixer(unrow(xr), w["norm_g"][1, 1], sh2, sc2, g2, sconv_prev, w["sconv_w"], w["w_in1"],
                          w["w_out1"], tm=t.sconv)
    sconv_new = buf[:, SUBLANES - (SCONV_WIDTH - 1):]
    y_out = unrow(ffn(as_rows(xm), 1, 1, sh3, sc3, g3, final_g=w["final_g"]))
    return y_out, states0, sconv_new


def kernel(x_prompt, x_sample, c_prompt, c_sample, cache_swa_k, cache_swa_v, state_ssd, state_ssd_conv, state_sconv, norm_g, w_ada, b_ada, w_ffn_gate, w_ffn_up, w_ffn_down, w_in_mix0, w_out_mix0, attn_sinks, ssd_conv_w, ssd_conv_b, ssd_dt_bias, ssd_a_log, ssd_d, ssd_norm_g, w_in_mix1, sconv_w, w_out_mix1, final_norm_g):
    bp, bs = c_prompt.shape[0], c_sample.shape[0]
    c_all = jnp.concatenate([c_prompt, c_sample], axis=0)
    c_rows = -(-c_all.shape[0] // SUBLANES) * SUBLANES
    c_all = jnp.pad(c_all, ((0, c_rows - c_all.shape[0]), (0, 0)))
    mod = ada_modulation(c_all, w_ada, b_ada)
    mods_p = [_split_mod(mod[l, :bp]) for l in range(mod.shape[0])]
    mods_s = [_split_mod(mod[l, bp:bp + bs]) for l in range(mod.shape[0])]

    weights = {
        "norm_g": norm_g,
        "ffn_gate": w_ffn_gate.astype(BF16), "ffn_up": w_ffn_up.astype(BF16), "ffn_down": w_ffn_down.astype(BF16),
        "w_in0": _arrange_w_in0(w_in_mix0[0]), "w_out0": w_out_mix0[0].astype(BF16),
        "sinks": attn_sinks[0],
        "ssd": (ssd_conv_w[0].astype(F32), ssd_conv_b[0].astype(F32), ssd_dt_bias[0], ssd_a_log[0], ssd_d[0],
                ssd_norm_g[0]),
        "w_in1": w_in_mix1[0].astype(BF16), "sconv_w": sconv_w[0], "w_out1": w_out_mix1[0].astype(BF16),
        "final_g": final_norm_g,
    }
    y_p, (k_p, v_p, h_p, cv_p), sc_p = _trunk(x_prompt, mods_p, weights, None)
    cache = (cache_swa_k[0], cache_swa_v[0], state_ssd[0], state_ssd_conv[0], state_sconv[0])
    y_s, (k_s, v_s, h_s, cv_s), sc_s = _trunk(x_sample, mods_s, weights, cache)
    stack = lambda t: t[None]
    return (y_p, y_s, stack(k_p), stack(v_p), stack(h_p), stack(cv_p), stack(sc_p),
            stack(k_s), stack(v_s), stack(h_s), stack(cv_s), stack(sc_s))
```

```python
import functools
from typing import NamedTuple

import numpy as np
import jax
import jax.numpy as jnp
from jax import lax
from jax.experimental import pallas as pl
from jax.experimental.pallas import tpu as pltpu

F32 = jnp.float32
BF16 = jnp.bfloat16

EPS = 1e-6
CHUNK = 64
PAST_LEN = 2048
ATTN_HEADS = 16
ATTN_KV_HEADS = 2
ATTN_GROUP = ATTN_HEADS // ATTN_KV_HEADS
HEAD_DIM = 64
WINDOW = 128
WINDOW_CHUNKS = WINDOW // CHUNK
ATTN_Q_DIM = ATTN_HEADS * HEAD_DIM
ATTN_KV_DIM = ATTN_KV_HEADS * HEAD_DIM
SSD_HEADS = 16
SSD_HEAD_DIM = 64
SSD_INNER = SSD_HEADS * SSD_HEAD_DIM
SSD_STATE = 128
SSD_GROUPS = 2
SSD_HPG = SSD_HEADS // SSD_GROUPS
SSD_CONV = 4
SSD_BC_DIM = 2 * SSD_GROUPS * SSD_STATE
SSD_CONV_DIM = SSD_INNER + SSD_BC_DIM
SCONV_WIDTH = 3
N_MOD = 9

LANES = 128
SUBLANES = 8
MXU_DIM = 256
VMEM_LIMIT = 62 * 1024 * 1024

PROJ_TN = 512
PA_Q = 0
PA_Z = PA_Q + ATTN_Q_DIM
PA_X = PA_Z + SSD_INNER
PA_BC = PA_X + SSD_INNER
PA_WIDTH = PA_BC + SSD_BC_DIM
PB_K = 0
PB_V = PB_K + ATTN_KV_DIM
PB_DT = PB_V + ATTN_KV_DIM
PB_WIDTH = PROJ_TN
ATTN_KEY_PAD = MXU_DIM
assert HEAD_DIM * 2 == LANES and ATTN_KV_DIM == LANES and PA_WIDTH % PROJ_TN == 0


def _params(*sem):
    return pltpu.CompilerParams(dimension_semantics=sem, vmem_limit_bytes=VMEM_LIMIT)


def _silu(x):
    return x / (1.0 + jnp.exp(-x))


def _softplus(x):
    return jnp.maximum(x, 0.0) + jnp.log(1.0 + jnp.exp(-jnp.abs(x)))


NORM_ROWS = 2 * SUBLANES


def _modnorm_store(h_ref, x_ref, g_ref, sc_ref, sh_ref, rs_scr):
    tm, d = x_ref.shape
    for c in range(tm // SUBLANES):
        r = pl.ds(c * SUBLANES, SUBLANES)
        x = x_ref[r, :]
        ms = jnp.mean(x * x, axis=-1, keepdims=True)
        rs_scr[r, :] = jnp.broadcast_to(lax.rsqrt(ms + EPS), (SUBLANES, LANES))
    per_row = sc_ref.shape[0] != 1
    scale = None if per_row else g_ref[...] * (1.0 + sc_ref[...])

    def body(c, carry):
        r = pl.ds(pl.multiple_of(c * NORM_ROWS, NORM_ROWS), NORM_ROWS)
        rs = jnp.concatenate([rs_scr[r, :]] * (d // LANES), axis=1)
        y = x_ref[r, :] * rs
        if per_row:
            h = y * (g_ref[...] * (1.0 + sc_ref[r, :])) + sh_ref[r, :]
        else:
            h = y * scale + sh_ref[...]
        h_ref[r, :] = h.astype(BF16)
        return carry

    lax.fori_loop(0, tm // NORM_ROWS, body, 0, unroll=2 if tm >= 2 * NORM_ROWS else 1)


def _mod_spec(mod_rows, d):
    if mod_rows == 1:
        return pl.BlockSpec((None, 1, d), lambda b, i, *_: (b, 0, 0))
    return pl.BlockSpec((None, mod_rows, d), lambda b, i, *_: (b, i, 0))


def _ada_kernel(c_ref, w_ref, b_ref, o_ref):
    c = c_ref[...]
    s = _silu(c).astype(BF16)
    o_ref[...] = jnp.dot(s, w_ref[...].astype(BF16), preferred_element_type=F32) + b_ref[...]


def ada_modulation(c, w_ada, b_ada, tn=1024):
    n_layers, d, n = w_ada.shape
    r = c.shape[0]
    return pl.pallas_call(
        _ada_kernel,
        grid=(n_layers, n // tn),
        in_specs=[
            pl.BlockSpec((r, d), lambda l, j: (0, 0)),
            pl.BlockSpec((None, d, tn), lambda l, j: (l, 0, j)),
            pl.BlockSpec((None, 1, tn), lambda l, j: (l, 0, j)),
        ],
        out_specs=pl.BlockSpec((None, r, tn), lambda l, j: (l, 0, j)),
        out_shape=jax.ShapeDtypeStruct((n_layers, r, n), F32),
        compiler_params=_params("arbitrary", "arbitrary"),
        name="ada_modulation",
    )(c, w_ada, b_ada.reshape(n_layers, 1, n))


def _residual_store(o_ref, x_ref, gt_ref, rs_scr, fg_ref):
    tm, d = x_ref.shape
    per_row = gt_ref.shape[0] != 1
    n_steps = tm // NORM_ROWS
    unroll = 2 if n_steps >= 2 else 1

    def residual(c, carry):
        r = pl.ds(pl.multiple_of(c * NORM_ROWS, NORM_ROWS), NORM_ROWS)
        gt = gt_ref[r, :] if per_row else gt_ref[...]
        o_ref[r, :] = x_ref[r, :] + 0.5 * gt * o_ref[r, :]
        return carry

    lax.fori_loop(0, n_steps, residual, 0, unroll=unroll)
    if fg_ref is None:
        return
    for c in range(tm // SUBLANES):
        r = pl.ds(c * SUBLANES, SUBLANES)
        xn = o_ref[r, :]
        ms = jnp.mean(xn * xn, axis=-1, keepdims=True)
        rs_scr[r, :] = jnp.broadcast_to(lax.rsqrt(ms + EPS), (SUBLANES, LANES))

    def scale(c, carry):
        r = pl.ds(pl.multiple_of(c * NORM_ROWS, NORM_ROWS), NORM_ROWS)
        rs = jnp.concatenate([rs_scr[r, :]] * (d // LANES), axis=1)
        o_ref[r, :] = o_ref[r, :] * rs * fg_ref[...]
        return carry

    lax.fori_loop(0, n_steps, scale, 0, unroll=unroll)


NORM_SPLIT = 8


def _modnorm_rows(h_ref, row0, x_ref, g_ref, sc_ref, sh_ref):
    rows = x_ref.shape[0]
    per_row = sc_ref.shape[0] != 1
    if not per_row:
        scale = g_ref[...] * (1.0 + sc_ref[...])
    tie = None
    for c in range(rows // NORM_ROWS):
        r = pl.ds(c * NORM_ROWS, NORM_ROWS)
        x = x_ref[r, :]
        y = x * lax.rsqrt(jnp.mean(x * x, axis=-1, keepdims=True) + EPS)
        if per_row:
            hv = y * (g_ref[...] * (1.0 + sc_ref[r, :])) + sh_ref[r, :]
        else:
            hv = y * scale + sh_ref[...]
        h_ref[pl.ds(row0 + c * NORM_ROWS, NORM_ROWS), :] = hv.astype(BF16)
        bits = pltpu.bitcast(hv[:, :LANES], jnp.uint32)
        zero = (bits >> 16) >> 16
        tie = zero if tie is None else tie | zero
    return tie


def _tied(v, tie):
    r, c = tie.shape
    head = v[:r, :c] + tie.astype(jnp.int32).astype(v.dtype)
    top = jnp.concatenate([head, v[:r, c:]], axis=1)
    return jnp.concatenate([top, v[r:]], axis=0)


def _next_tile(b, i, n_b, n_t):
    nxt = jnp.minimum(b * n_t + i + 1, n_b * n_t - 1)
    return nxt // n_t, nxt % n_t


def _ffn_kernel(x_ref, xn_ref, g_ref, sh_ref, sc_ref, gt_ref, wg_ref, wu_ref, wd_ref, *rest, n_ff, final):
    if final:
        fg_ref, o_ref, h_even, h_odd, rs_scr = rest
    else:
        fg_ref = None
        o_ref, h_even, h_odd, rs_scr = rest
    b, i, j = pl.program_id(0), pl.program_id(1), pl.program_id(2)
    n_b, n_t = pl.num_programs(0), pl.num_programs(1)
    tile = b * n_t + i
    tm, rows_n = x_ref.shape[0], xn_ref.shape[0]
    per_row = sc_ref.shape[1] != 1

    def mod_rows(ref, bb, row0, rows):
        return ref.at[bb, pl.ds(row0, rows)] if per_row else ref.at[bb]

    cur = functools.partial(mod_rows, bb=b, row0=pl.multiple_of(i * tm, tm), rows=tm)

    @pl.when((tile == 0) & (j == 0))
    def _():
        _modnorm_store(h_even, x_ref, g_ref, cur(sc_ref), cur(sh_ref), rs_scr)

    bn, tn = _next_tile(b, i, n_b, n_t)
    part = jnp.minimum(j, NORM_SPLIT - 1)
    nxt = functools.partial(mod_rows, bb=bn, row0=pl.multiple_of(tn * tm + part * rows_n, rows_n), rows=rows_n)

    def step(h_cur, h_nxt):
        tie = _modnorm_rows(h_nxt, pl.multiple_of(part * rows_n, rows_n), xn_ref, g_ref, nxt(sc_ref), nxt(sh_ref))
        h = h_cur[...]
        half = wg_ref.shape[1] // 2
        acts = []
        for c in range(2):
            cols = slice(c * half, (c + 1) * half)
            g = jnp.dot(h, wg_ref[:, cols], preferred_element_type=F32)
            u = jnp.dot(h, wu_ref[:, cols], preferred_element_type=F32)
            act = _silu(g) * u
            if c == 0:
                act = _tied(act, tie)
            acts.append(act.astype(BF16))
        total = jnp.where(j == 0, 0.0, o_ref[...])
        for c in range(2):
            total = total + jnp.dot(acts[c], wd_ref[c * half:(c + 1) * half, :], preferred_element_type=F32)
        o_ref[...] = total

    pl.when(tile % 2 == 0)(lambda: step(h_even, h_odd))
    pl.when(tile % 2 == 1)(lambda: step(h_odd, h_even))

    @pl.when(j == n_ff - 1)
    def _():
        _residual_store(o_ref, x_ref, cur(gt_ref), rs_scr, fg_ref)


def ffn_half(x, g, sh, sc, gt, wg, wu, wd, layer, half, final_g=None, *, tm, tf=512):
    b, s, d = x.shape
    f = wg.shape[-1]
    n_ff = f // tf
    n_t = s // tm
    assert n_ff >= NORM_SPLIT and tm % (NORM_SPLIT * NORM_ROWS) == 0
    rows_n = tm // NORM_SPLIT
    final = final_g is not None
    x_spec = pl.BlockSpec((None, tm, d), lambda b, i, j: (b, i, 0))

    def next_rows(bb, i, j):
        bn, tn = _next_tile(bb, i, b, n_t)
        return bn, tn * NORM_SPLIT + jnp.minimum(j, NORM_SPLIT - 1), 0

    whole = pl.BlockSpec(sh.shape, lambda b, i, j: (0, 0, 0))
    in_specs = [
        x_spec,
        pl.BlockSpec((None, rows_n, d), next_rows),
        pl.BlockSpec((1, d), lambda b, i, j: (0, 0)),
        whole, whole, whole,
        pl.BlockSpec((None, None, d, tf), lambda b, i, j: (layer, half, 0, j)),
        pl.BlockSpec((None, None, d, tf), lambda b, i, j: (layer, half, 0, j)),
        pl.BlockSpec((None, None, tf, d), lambda b, i, j: (layer, half, j, 0)),
    ]
    args = [x, x, g.reshape(1, d), sh, sc, gt, wg, wu, wd]
    if final:
        in_specs.append(pl.BlockSpec((1, d), lambda b, i, j: (0, 0)))
        args.append(final_g.reshape(1, d))
    return pl.pallas_call(
        functools.partial(_ffn_kernel, n_ff=n_ff, final=final),
        grid=(b, n_t, n_ff),
        in_specs=in_specs,
        out_specs=x_spec,
        out_shape=jax.ShapeDtypeStruct(x.shape, F32),
        scratch_shapes=[pltpu.VMEM((tm, d), BF16), pltpu.VMEM((tm, d), BF16), pltpu.VMEM((tm, LANES), F32)],
        compiler_params=_params("arbitrary", "arbitrary", "arbitrary"),
        name="ffn_half_final" if final else "ffn_half",
    )(*args)


def _proj_kernel(x_ref, g_ref, sh_ref, sc_ref, w_ref, *rest, n_main):
    j = pl.program_id(2)
    h_scr, rs_scr = rest[-2:]

    @pl.when(j == 0)
    def _():
        _modnorm_store(h_scr, x_ref, g_ref, sc_ref, sh_ref, rs_scr)

    r = jnp.dot(h_scr[...], w_ref[...], preferred_element_type=F32)
    if n_main is None:
        rest[0][...] = r.astype(rest[0].dtype)
    else:
        main_ref, tail_ref = rest[0], rest[1]

        @pl.when(j < n_main)
        def _():
            main_ref[...] = r.astype(main_ref.dtype)

        @pl.when(j >= n_main)
        def _():
            tail_ref[...] = r


def modnorm_proj(x, g, sh, sc, w, *, tm, tn=PROJ_TN, tail_width=None):
    b, s, d = x.shape
    n = w.shape[1]
    mod_rows = 1 if sh.shape[1] == 1 else tm
    if tail_width is None:
        n_main = None
        out_specs = pl.BlockSpec((None, tm, tn), lambda b, i, j: (b, i, j))
        out_shape = jax.ShapeDtypeStruct((b, s, n), BF16)
    else:
        assert tail_width == tn
        n_main = (n - tail_width) // tn
        out_specs = [
            pl.BlockSpec((None, tm, tn), lambda b, i, j: (b, i, jnp.minimum(j, n_main - 1))),
            pl.BlockSpec((None, tm, tn), lambda b, i, j: (b, i, 0)),
        ]
        out_shape = [jax.ShapeDtypeStruct((b, s, n - tail_width), BF16),
                     jax.ShapeDtypeStruct((b, s, tail_width), F32)]
    return pl.pallas_call(
        functools.partial(_proj_kernel, n_main=n_main),
        grid=(b, s // tm, n // tn),
        in_specs=[
            pl.BlockSpec((None, tm, d), lambda b, i, j: (b, i, 0)),
            pl.BlockSpec((1, d), lambda b, i, j: (0, 0)),
            _mod_spec(mod_rows, d), _mod_spec(mod_rows, d),
            pl.BlockSpec((d, tn), lambda b, i, j: (0, j)),
        ],
        out_specs=out_specs,
        out_shape=out_shape,
        scratch_shapes=[pltpu.VMEM((tm, d), BF16), pltpu.VMEM((tm, LANES), F32)],
        compiler_params=_params("arbitrary", "arbitrary", "arbitrary"),
        name="modnorm_proj",
    )(x, g.reshape(1, d), sh, sc, w)


def _kv_variants(t):
    low = lax.broadcasted_iota(jnp.int32, (1, LANES), 1) < HEAD_DIM
    swapped = pltpu.roll(t, HEAD_DIM, 1)
    zero = jnp.zeros_like(t)
    return {
        (0, 0): jnp.where(low, t, zero).astype(BF16), (0, 1): jnp.where(low, zero, swapped).astype(BF16),
        (1, 0): jnp.where(low, swapped, zero).astype(BF16), (1, 1): jnp.where(low, zero, t).astype(BF16),
    }


def _pad_rows(x, rows):
    return jnp.concatenate([x, jnp.zeros((rows - x.shape[0], x.shape[1]), x.dtype)], axis=0)


def _attn_rows(q_rows, k_var, v_var, bias_of):
    r = q_rows.shape[0]
    blocks_per_kv = ATTN_GROUP // 2
    low = lax.broadcasted_iota(jnp.int32, (ATTN_KEY_PAD, LANES), 1) < HEAD_DIM
    ones = (jnp.where(low, 1.0, 0.0).astype(BF16), jnp.where(low, 0.0, 1.0).astype(BF16))
    outs = []
    for h in range(ATTN_KV_HEADS):
        stack = jnp.concatenate(
            [q_rows[:, (h * blocks_per_kv + t) * LANES:(h * blocks_per_kv + t + 1) * LANES]
             for t in range(blocks_per_kv)], axis=0)
        stack = (stack.astype(F32) * (HEAD_DIM ** -0.5)).astype(BF16)
        res = None
        for half in range(2):
            keys = _pad_rows(k_var[h, half], ATTN_KEY_PAD)
            s = lax.dot_general(stack, keys, (((1,), (1,)), ((), ())), preferred_element_type=F32)
            s = s + bias_of(h, half)
            p = jnp.exp(s - jnp.max(s, axis=1, keepdims=True)).astype(BF16)
            rhs = jnp.concatenate([_pad_rows(v_var[h, half], ATTN_KEY_PAD), ones[half]], axis=1)
            part = jnp.dot(p, rhs, preferred_element_type=F32)
            res = part if res is None else res + part
        blk = res[:, :LANES] / res[:, LANES:]
        outs.extend(blk[t * r:(t + 1) * r] for t in range(blocks_per_kv))
    return jnp.concatenate(outs, axis=1)


def _alibi_slopes():
    return 2.0 ** (-8.0 * np.arange(1, ATTN_HEADS + 1) / ATTN_HEADS)


def _attn_bias(dist, valid, sinks):
    q, s = dist.shape
    blocks = ATTN_GROUP // 2
    order = lambda a: np.transpose(a.reshape((ATTN_KV_HEADS, blocks, 2) + a.shape[1:]), (0, 2, 1, 3, 4))
    base = np.where(valid[None], -_alibi_slopes()[:, None, None] * dist[None], -np.inf)
    base = order(base).reshape(ATTN_KV_HEADS, 2, blocks * q, s).astype(np.float32)
    sink = jnp.transpose(sinks.astype(F32).reshape(ATTN_KV_HEADS, blocks, 2), (0, 2, 1))
    sink = jnp.broadcast_to(sink[:, :, :, None, None], (ATTN_KV_HEADS, 2, blocks, q, 1))
    sink = sink.reshape(ATTN_KV_HEADS, 2, blocks * q, 1)
    tail = jnp.full((ATTN_KV_HEADS, 2, blocks * q, ATTN_KEY_PAD - s - 1), -jnp.inf, F32)
    return jnp.concatenate([jnp.asarray(base), sink, tail], axis=-1)


def _attn_prompt_kernel(q_ref, kvc_ref, kvp_ref, bias_ref, o_ref, *, n_chunks):
    i = pl.program_id(1)
    kv = jnp.concatenate([kvp_ref[...], kvc_ref[...]], axis=0)
    k_var = _kv_variants(kv[:, :ATTN_KV_DIM])
    v_var = _kv_variants(kv[:, ATTN_KV_DIM:])
    span = (WINDOW_CHUNKS + 1) * CHUNK
    for cc in range(n_chunks):
        variant = jnp.minimum(i * n_chunks + cc, WINDOW_CHUNKS)
        band = slice(cc * CHUNK, cc * CHUNK + span)
        rows = pl.ds(cc * CHUNK, CHUNK)
        o_ref[rows, :] = _attn_rows(
            q_ref[rows, :], {key: t[band] for key, t in k_var.items()}, {key: t[band] for key, t in v_var.items()},
            lambda h, half: bias_ref[variant, h, half]).astype(o_ref.dtype)


def attn_prompt(pa, pb, sinks, *, qb, out_dtype=BF16):
    b, s, _ = pa.shape
    n_chunks = qb // CHUNK
    span = (WINDOW_CHUNKS + 1) * CHUNK
    qi = np.arange(CHUNK)
    kj = np.arange(span)
    dist = np.abs(qi[:, None] + WINDOW_CHUNKS * CHUNK - kj[None, :]).astype(np.float64)
    bias = jnp.stack([
        _attn_bias(dist, np.broadcast_to(c - WINDOW_CHUNKS + kj[None, :] // CHUNK >= 0, dist.shape), sinks)
        for c in range(WINDOW_CHUNKS + 1)])
    kv_w = 2 * ATTN_KV_DIM
    return pl.pallas_call(
        functools.partial(_attn_prompt_kernel, n_chunks=n_chunks),
        grid=(b, s // qb),
        in_specs=[
            pl.BlockSpec((None, qb, ATTN_Q_DIM), lambda b, i: (b, i, PA_Q // ATTN_Q_DIM)),
            pl.BlockSpec((None, qb, kv_w), lambda b, i: (b, i, PB_K // kv_w)),
            pl.BlockSpec((None, WINDOW, kv_w), lambda b, i: (b, jnp.maximum(i * (qb // WINDOW) - 1, 0), PB_K // kv_w)),
            pl.BlockSpec(bias.shape, lambda b, i: (0,) * bias.ndim),
        ],
        out_specs=pl.BlockSpec((None, qb, ATTN_Q_DIM), lambda b, i: (b, i, 0)),
        out_shape=jax.ShapeDtypeStruct((b, s, ATTN_Q_DIM), out_dtype),
        compiler_params=_params("arbitrary", "arbitrary"),
        name="attn_prompt",
    )(pa, pb, pb, bias)


def _attn_sample_kernel(q_ref, kvn_ref, kc_ref, vc_ref, bias_ref, o_ref):
    kvn = kvn_ref[...]
    k_var = _kv_variants(jnp.concatenate([kc_ref[...], kvn[:, :ATTN_KV_DIM]], axis=0))
    v_var = _kv_variants(jnp.concatenate([vc_ref[...], kvn[:, ATTN_KV_DIM:]], axis=0))
    o_ref[...] = _attn_rows(q_ref[...], k_var, v_var, lambda h, half: bias_ref[h, half]).astype(o_ref.dtype)


def attn_sample(pa, pb, k_cache, v_cache, sinks, *, out_dtype=BF16):
    b, length, _ = pa.shape
    rows = k_cache.shape[1]
    qpos = PAST_LEN + np.arange(length)
    kpos = PAST_LEN - rows + np.arange(rows + length)
    qch, kch = qpos // CHUNK, kpos // CHUNK
    valid = (kch[None, :] <= qch[:, None]) & (kch[None, :] >= qch[:, None] - WINDOW_CHUNKS)
    dist = np.abs(qpos[:, None] - kpos[None, :]).astype(np.float64)
    bias = _attn_bias(dist, valid, sinks)
    kv_w = 2 * ATTN_KV_DIM
    cache = pl.BlockSpec((None, rows, ATTN_KV_DIM), lambda b: (b, 0, 0))
    return pl.pallas_call(
        _attn_sample_kernel,
        grid=(b,),
        in_specs=[
            pl.BlockSpec((None, length, ATTN_Q_DIM), lambda b: (b, 0, PA_Q // ATTN_Q_DIM)),
            pl.BlockSpec((None, length, kv_w), lambda b: (b, 0, PB_K // kv_w)),
            cache, cache,
            pl.BlockSpec(bias.shape, lambda b: (0,) * bias.ndim),
        ],
        out_specs=pl.BlockSpec((None, length, ATTN_Q_DIM), lambda b: (b, 0, 0)),
        out_shape=jax.ShapeDtypeStruct((b, length, ATTN_Q_DIM), out_dtype),
        compiler_params=_params("arbitrary"),
        name="attn_sample",
    )(pa, pb, k_cache, v_cache, bias)


def _causal_conv(pad_ref, prev_ref, cur, w_ref, b_ref, first, q):
    k = w_ref.shape[0]

    @pl.when(first)
    def _():
        pad_ref[pl.ds(0, SUBLANES), :] = prev_ref[...]

    pad_ref[pl.ds(SUBLANES, q), :] = cur
    acc = b_ref[...] + w_ref[k - 1:k, :] * cur
    for t in range(k - 1):
        acc = acc + w_ref[t:t + 1, :] * pad_ref[pl.ds(SUBLANES - (k - 1) + t, q), :]
    pad_ref[pl.ds(0, SUBLANES), :] = pad_ref[pl.ds(q, SUBLANES), :]
    return acc


def _split3(x):
    hi = x.astype(BF16)
    r = x - hi.astype(F32)
    mid = r.astype(BF16)
    lo = (r - mid.astype(F32)).astype(BF16)
    return hi, mid, lo


def _ssd_kernel(x_ref, bc_ref, z_ref, dt_ref, xprev_ref, bcprev_ref, h0_ref,
                wx_ref, bx_ref, wbc_ref, bbc_ref, dtb_ref, alog_ref, dskip_ref, ng_ref,
                y_ref, hout_ref, xpad, bcpad, h_scr, *, q, n_blocks, valid_len):
    i = pl.program_id(1)
    first = i == 0

    @pl.when(first)
    def _():
        h_scr[...] = h0_ref[...]

    xs = _silu(_causal_conv(xpad, xprev_ref, x_ref[...].astype(F32), wx_ref, bx_ref, first, q))
    bc = _silu(_causal_conv(bcpad, bcprev_ref, bc_ref[...].astype(F32), wbc_ref, bbc_ref, first, q))
    xs_b = xs.astype(BF16)

    dt = _softplus(dt_ref[...] + dtb_ref[...])
    if valid_len < q:
        row = lax.broadcasted_iota(jnp.int32, dt.shape, 0)
        dt = jnp.where(row < valid_len, dt, 0.0)
    da = dt * (-jnp.exp(alog_ref[...]))
    r_io = lax.broadcasted_iota(jnp.int32, (q, q), 0)
    c_io = lax.broadcasted_iota(jnp.int32, (q, q), 1)
    causal = c_io <= r_io
    tri = jnp.where(causal, 1.0, 0.0).astype(BF16)
    cum = sum(jnp.dot(tri, part, preferred_element_type=F32) for part in _split3(da))
    cum_t = cum.T
    dt_t = dt.T
    lane = lax.broadcasted_iota(jnp.int32, (1, LANES), 1)
    low_half = lane < SSD_HEAD_DIM

    y_pairs = []
    for g in range(SSD_GROUPS):
        b_g = bc[:, g * SSD_STATE:(g + 1) * SSD_STATE]
        c_g = bc[:, (SSD_GROUPS + g) * SSD_STATE:(SSD_GROUPS + g + 1) * SSD_STATE]
        cb = lax.dot_general(c_g.astype(BF16), b_g.astype(BF16), (((1,), (1,)), ((), ())),
                             preferred_element_type=F32)
        b_t = b_g.T
        for pr in range(g * SSD_HPG // 2, (g + 1) * SSD_HPG // 2):
            x_pair = xs_b[:, pr * LANES:(pr + 1) * LANES]
            h_pair = h_scr[pr]
            h_pair_b = h_pair.astype(BF16)
            ys, states, decays = [], [], []
            for j in (2 * pr, 2 * pr + 1):
                col = jnp.broadcast_to(cum[:, j:j + 1], (q, q))
                row = cum_t[j:j + 1, :]
                decay_in = jnp.exp(jnp.where(causal, col - row, -jnp.inf))
                m = (cb * decay_in * dt_t[j:j + 1, :]).astype(BF16)
                c_scaled = (c_g * jnp.exp(col[:, :SSD_STATE])).astype(BF16)
                ys.append(jnp.dot(m, x_pair, preferred_element_type=F32)
                          + jnp.dot(c_scaled, h_pair_b, preferred_element_type=F32))
                last = row[:, q - 1:q]
                w_row = dt_t[j:j + 1, :] * jnp.exp(last - row)
                states.append(jnp.dot((b_t * w_row).astype(BF16), x_pair, preferred_element_type=F32))
                decays.append(jnp.exp(last))
            h_scr[pr] = (jnp.where(low_half, decays[0], decays[1]) * h_pair
                         + jnp.where(low_half, states[0], states[1]))
            y_pairs.append(jnp.where(low_half, ys[0], ys[1]))
    y = jnp.concatenate(y_pairs, axis=1) + dskip_ref[...] * xs
    y = y * _silu(z_ref[...].astype(F32))
    half = SSD_INNER // SSD_GROUPS
    normed = []
    for g in range(SSD_GROUPS):
        yg = y[:, g * half:(g + 1) * half]
        normed.append(yg * lax.rsqrt(jnp.mean(yg * yg, axis=-1, keepdims=True) + EPS))
    y_ref[...] = (jnp.concatenate(normed, axis=1) * ng_ref[...]).astype(y_ref.dtype)

    @pl.when(i == n_blocks - 1)
    def _():
        hout_ref[...] = h_scr[...]


def ssd_mixer(pa, pb, x_prev, bc_prev, h0, conv_w, conv_b, dt_bias, a_log, d_skip, norm_g, *, q, valid_len=None,
              out_dtype=BF16):
    b, s, _ = pa.shape
    n_blocks = s // q
    valid_len = q if valid_len is None else valid_len
    n_pairs = SSD_HEADS // 2
    pad128 = lambda v: jnp.pad(v.astype(F32), (0, LANES - v.shape[0])).reshape(1, LANES)
    col = lambda width, off: pl.BlockSpec((None, q, width), lambda b, i: (b, i, off // width))
    const = lambda shape: pl.BlockSpec(shape, lambda b, i: (0,) * len(shape))
    per_b = lambda shape: pl.BlockSpec((None,) + shape, lambda b, i: (b,) + (0,) * len(shape))
    h_shape = (n_pairs, SSD_STATE, LANES)
    return pl.pallas_call(
        functools.partial(_ssd_kernel, q=q, n_blocks=n_blocks, valid_len=valid_len),
        grid=(b, n_blocks),
        in_specs=[
            col(SSD_INNER, PA_X), col(SSD_BC_DIM, PA_BC), col(SSD_INNER, PA_Z), col(LANES, PB_DT),
            per_b((SUBLANES, SSD_INNER)), per_b((SUBLANES, SSD_BC_DIM)), per_b(h_shape),
            const((SSD_CONV, SSD_INNER)), const((1, SSD_INNER)),
            const((SSD_CONV, SSD_BC_DIM)), const((1, SSD_BC_DIM)),
            const((1, LANES)), const((1, LANES)), const((1, SSD_INNER)), const((1, SSD_INNER)),
        ],
        out_specs=[
            pl.BlockSpec((None, q, SSD_INNER), lambda b, i: (b, i, 0)),
            per_b(h_shape),
        ],
        out_shape=[
            jax.ShapeDtypeStruct((b, s, SSD_INNER), out_dtype),
            jax.ShapeDtypeStruct((b,) + h_shape, F32),
        ],
        scratch_shapes=[
            pltpu.VMEM((q + SUBLANES, SSD_INNER), F32),
            pltpu.VMEM((q + SUBLANES, SSD_BC_DIM), F32),
            pltpu.VMEM(h_shape, F32),
        ],
        compiler_params=_params("arbitrary", "arbitrary"),
        name="ssd_mixer",
    )(pa, pa, pa, pb, x_prev, bc_prev, h0,
      conv_w[:, :SSD_INNER], conv_b[:SSD_INNER].reshape(1, -1),
      conv_w[:, SSD_INNER:], conv_b[SSD_INNER:].reshape(1, -1),
      pad128(dt_bias), pad128(a_log),
      jnp.repeat(d_skip.astype(F32), SSD_HEAD_DIM).reshape(1, -1), norm_g.astype(F32).reshape(1, -1))


def _pair_state(h):
    b = h.shape[0]
    h = h.astype(F32).reshape(b, SSD_HEADS // 2, 2, SSD_HEAD_DIM, SSD_STATE)
    return h.transpose(0, 1, 4, 2, 3).reshape(b, SSD_HEADS // 2, SSD_STATE, 2 * SSD_HEAD_DIM)


def _unpair_state(h):
    b = h.shape[0]
    h = h.reshape(b, SSD_HEADS // 2, SSD_STATE, 2, SSD_HEAD_DIM)
    return h.transpose(0, 1, 3, 4, 2).reshape(b, SSD_HEADS, SSD_HEAD_DIM, SSD_STATE)


def _tail_rows(prev):
    return jnp.pad(prev.astype(F32), ((0, 0), (SUBLANES - prev.shape[1], 0), (0, 0)))


def _outproj_kernel(x_ref, a_ref, y_ref, gt_ref, wa_ref, wy_ref, o_ref):
    m = (jnp.dot(a_ref[...].astype(BF16), wa_ref[...], preferred_element_type=F32)
         + jnp.dot(y_ref[...].astype(BF16), wy_ref[...], preferred_element_type=F32))
    o_ref[...] = x_ref[...] + gt_ref[...] * m


def mix0_out(x, attn, y, gt, w_out, *, tm):
    b, s, d = x.shape
    da, dy = attn.shape[2], y.shape[2]
    mod_rows = 1 if gt.shape[1] == 1 else tm
    x_spec = pl.BlockSpec((None, tm, d), lambda b, i: (b, i, 0))
    return pl.pallas_call(
        _outproj_kernel,
        grid=(b, s // tm),
        in_specs=[
            x_spec,
            pl.BlockSpec((None, tm, da), lambda b, i: (b, i, 0)),
            pl.BlockSpec((None, tm, dy), lambda b, i: (b, i, 0)),
            _mod_spec(mod_rows, d),
            pl.BlockSpec((da, d), lambda b, i: (0, 0)),
            pl.BlockSpec((dy, d), lambda b, i: (1, 0)),
        ],
        out_specs=x_spec,
        out_shape=jax.ShapeDtypeStruct(x.shape, F32),
        compiler_params=_params("arbitrary", "arbitrary"),
        name="mix0_out",
    )(x, attn, y, gt, w_out, w_out)


SCONV_TC = 512


def _sconv_kernel(x_ref, g_ref, sh_ref, sc_ref, gt_ref, prev_ref, cw_ref, win_ref, wout_ref,
                  o_ref, buf_ref, h_scr, rs_scr, vpad, carry, *, tm):
    @pl.when(pl.program_id(1) == 0)
    def _():
        carry[...] = prev_ref[...]

    _modnorm_store(h_scr, x_ref, g_ref, sc_ref, sh_ref, rs_scr)
    h = h_scr[...]
    d = x_ref.shape[1]
    k = cw_ref.shape[0]
    tc = SCONV_TC
    acc = None
    for c in range(d // tc):
        cols = slice(c * tc, (c + 1) * tc)
        gate_b, gate_c, xi = (jnp.dot(h, win_ref[:, part * d + c * tc:part * d + (c + 1) * tc],
                                      preferred_element_type=F32) for part in range(3))
        v = gate_c * xi
        vpad[pl.ds(0, SUBLANES), cols] = carry[:, cols]
        vpad[pl.ds(SUBLANES, tm), cols] = v
        u = cw_ref[k - 1:k, cols] * v
        for t in range(k - 1):
            u = u + cw_ref[t:t + 1, cols] * vpad[pl.ds(SUBLANES - (k - 1) + t, tm), cols]
        carry[:, cols] = vpad[pl.ds(tm, SUBLANES), cols]
        part = jnp.dot((gate_b * u).astype(BF16), wout_ref[cols, :], preferred_element_type=F32)
        acc = part if acc is None else acc + part
    o_ref[...] = x_ref[...] + gt_ref[...] * acc
    buf_ref[...] = carry[...]


def sconv_mixer(x, g, sh, sc, gt, prev, conv_w, w_in, w_out, *, tm):
    b, s, d = x.shape
    x_spec = pl.BlockSpec((None, tm, d), lambda b, i: (b, i, 0))
    buf_spec = pl.BlockSpec((None, SUBLANES, d), lambda b, i: (b, 0, 0))
    const = lambda shape: pl.BlockSpec(shape, lambda b, i: (0, 0))
    resident = lambda shape: pl.BlockSpec(shape, lambda b, i: (0, 0), pipeline_mode=pl.Buffered(1))
    return pl.pallas_call(
        functools.partial(_sconv_kernel, tm=tm),
        grid=(b, s // tm),
        in_specs=[
            x_spec, const((1, d)), _mod_spec(1, d), _mod_spec(1, d), _mod_spec(1, d), buf_spec,
            const((SCONV_WIDTH, d)), resident((d, 3 * d)), resident((d, d)),
        ],
        out_specs=[x_spec, buf_spec],
        out_shape=[jax.ShapeDtypeStruct(x.shape, F32), jax.ShapeDtypeStruct((b, SUBLANES, d), F32)],
        scratch_shapes=[
            pltpu.VMEM((tm, d), BF16), pltpu.VMEM((tm, LANES), F32),
            pltpu.VMEM((tm + SUBLANES, d), F32), pltpu.VMEM((SUBLANES, d), F32),
        ],
        compiler_params=_params("arbitrary", "arbitrary"),
        name="sconv_mixer",
    )(x, g.reshape(1, d), sh, sc, gt, prev, conv_w.astype(F32), w_in, w_out)


def _arrange_w_in0(w):
    i1 = ATTN_Q_DIM
    i2 = i1 + ATTN_KV_DIM
    i3 = i2 + ATTN_KV_DIM
    i4 = i3 + SSD_INNER
    i5 = i4 + SSD_CONV_DIM
    q, k, v, z, xbc, dt = (w[:, a:b] for a, b in ((0, i1), (i1, i2), (i2, i3), (i3, i4), (i4, i5), (i5, w.shape[1])))
    pad = jnp.zeros((w.shape[0], PB_WIDTH - PB_DT - dt.shape[1]), w.dtype)
    return jnp.concatenate([q, z, xbc, k, v, dt, pad], axis=1).astype(BF16)


def _split_mod(mod):
    return [m[:, None, :] for m in jnp.split(mod, N_MOD, axis=-1)]


def _per_token(m, length):
    b, _, d = m.shape
    return jnp.broadcast_to(m, (b, length, d)).reshape(1, b * length, d)


class _Tiles(NamedTuple):
    ffn: int
    proj: int
    mix_out: int
    sconv: int
    ssd: int
    attn: int


def _tiles(batch, seq, flat):
    if flat:
        rows = batch * seq
        return _Tiles(ffn=rows, proj=rows, mix_out=rows, sconv=seq, ssd=LANES, attn=seq)
    return _Tiles(ffn=min(1024, seq), proj=min(1024, seq), mix_out=min(512, seq), sconv=min(256, seq),
                  ssd=min(256, seq), attn=min(256, seq))


def _trunk(x, mods, weights, cache):
    b, s, d = x.shape
    t = _tiles(b, s, cache is not None)
    tm, tm_proj, tm_mix, q_ssd, qb_attn = t.ffn, t.proj, t.mix_out, t.ssd, t.attn
    flat = cache is not None
    if flat:
        as_rows = lambda t: t.reshape(1, b * s, t.shape[-1])
        mod_of = lambda m: _per_token(m, s)
    else:
        as_rows = lambda t: t
        mod_of = lambda m: m
    unrow = lambda t: t.reshape(b, s, t.shape[-1])
    w = weights

    def ffn(xr, l, k, sh, sc, gt, final_g=None):
        return ffn_half(xr, w["norm_g"][l, 2 * k], mod_of(sh), mod_of(sc), mod_of(gt),
                        w["ffn_gate"], w["ffn_up"], w["ffn_down"], l, k, final_g, tm=tm)

    sh1, sc1, g1, sh2, sc2, g2, sh3, sc3, g3 = mods[0]
    xr = ffn(as_rows(x), 0, 0, sh1, sc1, g1)
    pa, pb = modnorm_proj(xr, w["norm_g"][0, 1], mod_of(sh2), mod_of(sc2), w["w_in0"], tm=tm_proj,
                          tail_width=PB_WIDTH)
    pa, pb = unrow(pa), unrow(pb)
    if cache is None:
        attn = attn_prompt(pa, pb, w["sinks"], qb=qb_attn)
        x_prev = jnp.zeros((b, SUBLANES, SSD_INNER), F32)
        bc_prev = jnp.zeros((b, SUBLANES, SSD_BC_DIM), F32)
        h0 = jnp.zeros((b, SSD_HEADS // 2, SSD_STATE, LANES), F32)
        y, h_new = ssd_mixer(pa, pb, x_prev, bc_prev, h0, *w["ssd"], q=q_ssd)
        new_kv = pb[:, s - WINDOW:]
        sconv_prev = jnp.zeros((b, SUBLANES, d), F32)
    else:
        k_cache, v_cache, h_state, conv_prev, sconv_state = cache
        rows = k_cache.shape[1]
        attn = attn_sample(pa, pb, k_cache.reshape(b, rows, ATTN_KV_DIM), v_cache.reshape(b, rows, ATTN_KV_DIM),
                           w["sinks"])
        tail = _tail_rows(conv_prev)
        pad_rows = lambda t: jnp.pad(t, ((0, 0), (0, q_ssd - s), (0, 0)))
        y, h_new = ssd_mixer(pad_rows(pa), pad_rows(pb), tail[:, :, :SSD_INNER], tail[:, :, SSD_INNER:],
                             _pair_state(h_state), *w["ssd"], q=q_ssd, valid_len=s)
        y = y[:, :s]
        new_kv = pb
        sconv_prev = _tail_rows(sconv_state)
    xbc_raw = pa[:, s - (SSD_CONV - 1):, PA_X:PA_X + SSD_CONV_DIM].astype(F32)
    kv_shape = (b, new_kv.shape[1], ATTN_KV_HEADS, HEAD_DIM)
    states0 = (new_kv[:, :, PB_K:PB_K + ATTN_KV_DIM].reshape(kv_shape),
               new_kv[:, :, PB_V:PB_V + ATTN_KV_DIM].reshape(kv_shape), _unpair_state(h_new), xbc_raw)
    xr = mix0_out(xr, as_rows(attn), as_rows(y), mod_of(g2), w["w_out0"], tm=tm_mix)
    xr = ffn(xr, 0, 1, sh3, sc3, g3)

    sh1, sc1, g1, sh2, sc2, g2, sh3, sc3, g3 = mods[1]
    xr = ffn(xr, 1, 0, sh1, sc1, g1)
    xm, buf = sconv_mixer(unrow(xr), w["norm_g"][1, 1], sh2, sc2, g2, sconv_prev, w["sconv_w"], w["w_in1"],
                          w["w_out1"], tm=t.sconv)
    sconv_new = buf[:, SUBLANES - (SCONV_WIDTH - 1):]
    y_out = unrow(ffn(as_rows(xm), 1, 1, sh3, sc3, g3, final_g=w["final_g"]))
    return y_out, states0, sconv_new


def kernel(x_prompt, x_sample, c_prompt, c_sample, cache_swa_k, cache_swa_v, state_ssd, state_ssd_conv, state_sconv, norm_g, w_ada, b_ada, w_ffn_gate, w_ffn_up, w_ffn_down, w_in_mix0, w_out_mix0, attn_sinks, ssd_conv_w, ssd_conv_b, ssd_dt_bias, ssd_a_log, ssd_d, ssd_norm_g, w_in_mix1, sconv_w, w_out_mix1, final_norm_g):
    bp, bs = c_prompt.shape[0], c_sample.shape[0]
    c_all = jnp.concatenate([c_prompt, c_sample], axis=0)
    c_rows = -(-c_all.shape[0] // SUBLANES) * SUBLANES
    c_all = jnp.pad(c_all, ((0, c_rows - c_all.shape[0]), (0, 0)))
    mod = ada_modulation(c_all, w_ada, b_ada)
    mods_p = [_split_mod(mod[l, :bp]) for l in range(mod.shape[0])]
    mods_s = [_split_mod(mod[l, bp:bp + bs]) for l in range(mod.shape[0])]

    weights = {
        "norm_g": norm_g,
        "ffn_gate": w_ffn_gate.astype(BF16), "ffn_up": w_ffn_up.astype(BF16), "ffn_down": w_ffn_down.astype(BF16),
        "w_in0": _arrange_w_in0(w_in_mix0[0]), "w_out0": w_out_mix0[0].astype(BF16),
        "sinks": attn_sinks[0],
        "ssd": (ssd_conv_w[0].astype(F32), ssd_conv_b[0].astype(F32), ssd_dt_bias[0], ssd_a_log[0], ssd_d[0],
                ssd_norm_g[0]),
        "w_in1": w_in_mix1[0].astype(BF16), "sconv_w": sconv_w[0], "w_out1": w_out_mix1[0].astype(BF16),
        "final_g": final_norm_g,
    }
    y_p, (k_p, v_p, h_p, cv_p), sc_p = _trunk(x_prompt, mods_p, weights, None)
    cache = (cache_swa_k[0], cache_swa_v[0], state_ssd[0], state_ssd_conv[0], state_sconv[0])
    y_s, (k_s, v_s, h_s, cv_s), sc_s = _trunk(x_sample, mods_s, weights, cache)
    stack = lambda t: t[None]
    return (y_p, y_s, stack(k_p), stack(v_p), stack(h_p), stack(cv_p), stack(sc_p),
            stack(k_s), stack(v_s), stack(h_s), stack(cv_s), stack(sc_s))
```

```python
import functools
from typing import NamedTuple

import numpy as np
import jax
import jax.numpy as jnp
from jax import lax
from jax.experimental import pallas as pl
from jax.experimental.pallas import tpu as pltpu

F32 = jnp.float32
BF16 = jnp.bfloat16

EPS = 1e-6
CHUNK = 64
PAST_LEN = 2048
ATTN_HEADS = 16
ATTN_KV_HEADS = 2
ATTN_GROUP = ATTN_HEADS // ATTN_KV_HEADS
HEAD_DIM = 64
WINDOW = 128
WINDOW_CHUNKS = WINDOW // CHUNK
ATTN_Q_DIM = ATTN_HEADS * HEAD_DIM
ATTN_KV_DIM = ATTN_KV_HEADS * HEAD_DIM
SSD_HEADS = 16
SSD_HEAD_DIM = 64
SSD_INNER = SSD_HEADS * SSD_HEAD_DIM
SSD_STATE = 128
SSD_GROUPS = 2
SSD_HPG = SSD_HEADS // SSD_GROUPS
SSD_CONV = 4
SSD_BC_DIM = 2 * SSD_GROUPS * SSD_STATE
SSD_CONV_DIM = SSD_INNER + SSD_BC_DIM
SCONV_WIDTH = 3
N_MOD = 9

LANES = 128
SUBLANES = 8
MXU_DIM = 256
VMEM_LIMIT = 62 * 1024 * 1024

PROJ_TN = 512
PA_Q = 0
PA_Z = PA_Q + ATTN_Q_DIM
PA_X = PA_Z + SSD_INNER
PA_BC = PA_X + SSD_INNER
PA_WIDTH = PA_BC + SSD_BC_DIM
PB_K = 0
PB_V = PB_K + ATTN_KV_DIM
PB_DT = PB_V + ATTN_KV_DIM
PB_WIDTH = PROJ_TN
ATTN_KEY_PAD = MXU_DIM
assert HEAD_DIM * 2 == LANES and ATTN_KV_DIM == LANES and PA_WIDTH % PROJ_TN == 0


def _params(*sem):
    return pltpu.CompilerParams(dimension_semantics=sem, vmem_limit_bytes=VMEM_LIMIT)


def _silu(x):
    return x / (1.0 + jnp.exp(-x))


def _softplus(x):
    return jnp.maximum(x, 0.0) + jnp.log(1.0 + jnp.exp(-jnp.abs(x)))


NORM_ROWS = 2 * SUBLANES


def _modnorm_store(h_ref, x_ref, g_ref, sc_ref, sh_ref, rs_scr):
    tm, d = x_ref.shape
    for c in range(tm // SUBLANES):
        r = pl.ds(c * SUBLANES, SUBLANES)
        x = x_ref[r, :]
        ms = jnp.mean(x * x, axis=-1, keepdims=True)
        rs_scr[r, :] = jnp.broadcast_to(lax.rsqrt(ms + EPS), (SUBLANES, LANES))
    per_row = sc_ref.shape[0] != 1
    scale = None if per_row else g_ref[...] * (1.0 + sc_ref[...])

    def body(c, carry):
        r = pl.ds(pl.multiple_of(c * NORM_ROWS, NORM_ROWS), NORM_ROWS)
        rs = jnp.concatenate([rs_scr[r, :]] * (d // LANES), axis=1)
        y = x_ref[r, :] * rs
        if per_row:
            h = y * (g_ref[...] * (1.0 + sc_ref[r, :])) + sh_ref[r, :]
        else:
            h = y * scale + sh_ref[...]
        h_ref[r, :] = h.astype(BF16)
        return carry

    lax.fori_loop(0, tm // NORM_ROWS, body, 0, unroll=2 if tm >= 2 * NORM_ROWS else 1)


def _mod_spec(mod_rows, d):
    if mod_rows == 1:
        return pl.BlockSpec((None, 1, d), lambda b, i, *_: (b, 0, 0))
    return pl.BlockSpec((None, mod_rows, d), lambda b, i, *_: (b, i, 0))


def _ada_kernel(c_ref, w_ref, b_ref, o_ref):
    c = c_ref[...]
    s = _silu(c).astype(BF16)
    o_ref[...] = jnp.dot(s, w_ref[...].astype(BF16), preferred_element_type=F32) + b_ref[...]


def ada_modulation(c, w_ada, b_ada, tn=1024):
    n_layers, d, n = w_ada.shape
    r = c.shape[0]
    return pl.pallas_call(
        _ada_kernel,
        grid=(n_layers, n // tn),
        in_specs=[
            pl.BlockSpec((r, d), lambda l, j: (0, 0)),
            pl.BlockSpec((None, d, tn), lambda l, j: (l, 0, j)),
            pl.BlockSpec((None, 1, tn), lambda l, j: (l, 0, j)),
        ],
        out_specs=pl.BlockSpec((None, r, tn), lambda l, j: (l, 0, j)),
        out_shape=jax.ShapeDtypeStruct((n_layers, r, n), F32),
        compiler_params=_params("arbitrary", "arbitrary"),
        name="ada_modulation",
    )(c, w_ada, b_ada.reshape(n_layers, 1, n))


def _residual_store(o_ref, x_ref, gt_ref, rs_scr, fg_ref):
    tm, d = x_ref.shape
    per_row = gt_ref.shape[0] != 1
    n_steps = tm // NORM_ROWS
    unroll = 2 if n_steps >= 2 else 1

    def residual(c, carry):
        r = pl.ds(pl.multiple_of(c * NORM_ROWS, NORM_ROWS), NORM_ROWS)
        gt = gt_ref[r, :] if per_row else gt_ref[...]
        o_ref[r, :] = x_ref[r, :] + 0.5 * gt * o_ref[r, :]
        return carry

    lax.fori_loop(0, n_steps, residual, 0, unroll=unroll)
    if fg_ref is None:
        return
    for c in range(tm // SUBLANES):
        r = pl.ds(c * SUBLANES, SUBLANES)
        xn = o_ref[r, :]
        ms = jnp.mean(xn * xn, axis=-1, keepdims=True)
        rs_scr[r, :] = jnp.broadcast_to(lax.rsqrt(ms + EPS), (SUBLANES, LANES))

    def scale(c, carry):
        r = pl.ds(pl.multiple_of(c * NORM_ROWS, NORM_ROWS), NORM_ROWS)
        rs = jnp.concatenate([rs_scr[r, :]] * (d // LANES), axis=1)
        o_ref[r, :] = o_ref[r, :] * rs * fg_ref[...]
        return carry

    lax.fori_loop(0, n_steps, scale, 0, unroll=unroll)


NORM_SPLIT = 8


def _modnorm_rows(h_ref, row0, x_ref, g_ref, sc_ref, sh_ref):
    rows = x_ref.shape[0]
    per_row = sc_ref.shape[0] != 1
    if not per_row:
        scale = g_ref[...] * (1.0 + sc_ref[...])
    tie = None
    for c in range(rows // NORM_ROWS):
        r = pl.ds(c * NORM_ROWS, NORM_ROWS)
        x = x_ref[r, :]
        y = x * lax.rsqrt(jnp.mean(x * x, axis=-1, keepdims=True) + EPS)
        if per_row:
            hv = y * (g_ref[...] * (1.0 + sc_ref[r, :])) + sh_ref[r, :]
        else:
            hv = y * scale + sh_ref[...]
        h_ref[pl.ds(row0 + c * NORM_ROWS, NORM_ROWS), :] = hv.astype(BF16)
        zero = _zero_tie(hv)
        tie = zero if tie is None else tie | zero
    return tie


def _zero_tie(v):
    bits = pltpu.bitcast(v[:NORM_ROWS, :LANES], jnp.uint32)
    return (bits >> 16) >> 16


def _tied(v, tie):
    r, c = tie.shape
    head = v[:r, :c] + tie.astype(jnp.int32).astype(F32).astype(v.dtype)
    top = jnp.concatenate([head, v[:r, c:]], axis=1)
    return top if v.shape[0] == r else jnp.concatenate([top, v[r:]], axis=0)


def _next_tile(b, i, n_b, n_t):
    nxt = jnp.minimum(b * n_t + i + 1, n_b * n_t - 1)
    return nxt // n_t, nxt % n_t


def _ffn_kernel(x_ref, xn_ref, g_ref, sh_ref, sc_ref, gt_ref, wg_ref, wu_ref, wd_ref, *rest, n_ff, final):
    if final:
        fg_ref, o_ref, h_even, h_odd, rs_scr = rest
    else:
        fg_ref = None
        o_ref, h_even, h_odd, rs_scr = rest
    b, i, j = pl.program_id(0), pl.program_id(1), pl.program_id(2)
    n_b, n_t = pl.num_programs(0), pl.num_programs(1)
    tile = b * n_t + i
    tm, rows_n = x_ref.shape[0], xn_ref.shape[0]
    per_row = sc_ref.shape[1] != 1

    def mod_rows(ref, bb, row0, rows):
        return ref.at[bb, pl.ds(row0, rows)] if per_row else ref.at[bb]

    cur = functools.partial(mod_rows, bb=b, row0=pl.multiple_of(i * tm, tm), rows=tm)

    @pl.when((tile == 0) & (j == 0))
    def _():
        _modnorm_store(h_even, x_ref, g_ref, cur(sc_ref), cur(sh_ref), rs_scr)

    bn, tn = _next_tile(b, i, n_b, n_t)
    part = jnp.minimum(j, NORM_SPLIT - 1)
    nxt = functools.partial(mod_rows, bb=bn, row0=pl.multiple_of(tn * tm + part * rows_n, rows_n), rows=rows_n)

    def step(h_cur, h_nxt, last):
        tie = _modnorm_rows(h_nxt, pl.multiple_of(part * rows_n, rows_n), xn_ref, g_ref, nxt(sc_ref), nxt(sh_ref))
        h = h_cur[...]
        half = wg_ref.shape[1] // 2
        acts = []
        for c in range(2):
            cols = slice(c * half, (c + 1) * half)
            g = jnp.dot(h, wg_ref[:, cols], preferred_element_type=F32)
            u = jnp.dot(h, wu_ref[:, cols], preferred_element_type=F32)
            act = _silu(g) * u
            if c == 0:
                act = _tied(act, tie)
            acts.append(act.astype(BF16))
        total = jnp.where(j == 0, 0.0, o_ref[...])
        for c in range(2):
            total = total + jnp.dot(acts[c], wd_ref[c * half:(c + 1) * half, :], preferred_element_type=F32)
        if last and not final:
            total = x_ref[...] + 0.5 * cur(gt_ref)[...] * total
        o_ref[...] = total

    for parity, (h_cur, h_nxt) in enumerate(((h_even, h_odd), (h_odd, h_even))):
        if final:
            pl.when(tile % 2 == parity)(functools.partial(step, h_cur, h_nxt, False))
        else:
            pl.when((tile % 2 == parity) & (j < n_ff - 1))(functools.partial(step, h_cur, h_nxt, False))
            pl.when((tile % 2 == parity) & (j == n_ff - 1))(functools.partial(step, h_cur, h_nxt, True))

    if final:
        @pl.when(j == n_ff - 1)
        def _():
            _residual_store(o_ref, x_ref, cur(gt_ref), rs_scr, fg_ref)


FFN_TF = 512


def _chunk_major(w):
    l, h, d, f = w.shape
    return jnp.transpose(w.reshape(l, h, d, f // FFN_TF, FFN_TF), (0, 1, 3, 2, 4)).astype(BF16)


def ffn_half(x, g, sh, sc, gt, wg, wu, wd, layer, half, final_g=None, *, tm):
    b, s, d = x.shape
    n_ff, tf = wg.shape[2], wg.shape[4]
    n_t = s // tm
    assert n_ff >= NORM_SPLIT and tm % (NORM_SPLIT * NORM_ROWS) == 0
    rows_n = tm // NORM_SPLIT
    final = final_g is not None
    x_spec = pl.BlockSpec((None, tm, d), lambda b, i, j: (b, i, 0))

    def next_rows(bb, i, j):
        bn, tn = _next_tile(bb, i, b, n_t)
        return bn, tn * NORM_SPLIT + jnp.minimum(j, NORM_SPLIT - 1), 0

    whole = pl.BlockSpec(sh.shape, lambda b, i, j: (0, 0, 0))
    in_specs = [
        x_spec,
        pl.BlockSpec((None, rows_n, d), next_rows),
        pl.BlockSpec((1, d), lambda b, i, j: (0, 0)),
        whole, whole, whole,
        pl.BlockSpec((None, None, None, d, tf), lambda b, i, j: (layer, half, j, 0, 0)),
        pl.BlockSpec((None, None, None, d, tf), lambda b, i, j: (layer, half, j, 0, 0)),
        pl.BlockSpec((None, None, tf, d), lambda b, i, j: (layer, half, j, 0)),
    ]
    args = [x, x, g.reshape(1, d), sh, sc, gt, wg, wu, wd]
    if final:
        in_specs.append(pl.BlockSpec((1, d), lambda b, i, j: (0, 0)))
        args.append(final_g.reshape(1, d))
    return pl.pallas_call(
        functools.partial(_ffn_kernel, n_ff=n_ff, final=final),
        grid=(b, n_t, n_ff),
        in_specs=in_specs,
        out_specs=x_spec,
        out_shape=jax.ShapeDtypeStruct(x.shape, F32),
        scratch_shapes=[pltpu.VMEM((tm, d), BF16), pltpu.VMEM((tm, d), BF16), pltpu.VMEM((tm, LANES), F32)],
        compiler_params=_params("arbitrary", "arbitrary", "arbitrary"),
        name="ffn_half_final" if final else "ffn_half",
    )(*args)


def _proj_kernel(x0_ref, xn_ref, g_ref, sh_ref, sc_ref, w_ref, main_ref, tail_ref, h_even, h_odd, rs_scr):
    b, i = pl.program_id(0), pl.program_id(1)
    n_b, n_t = pl.num_programs(0), pl.num_programs(1)
    tile = b * n_t + i
    tm = xn_ref.shape[0]
    per_row = sc_ref.shape[1] != 1

    def mod_rows(ref, bb, tt):
        return ref.at[bb, pl.ds(pl.multiple_of(tt * tm, tm), tm)] if per_row else ref.at[bb]

    @pl.when(tile == 0)
    def _():
        _modnorm_store(h_even, x0_ref, g_ref, mod_rows(sc_ref, 0, 0), mod_rows(sh_ref, 0, 0), rs_scr)

    bn, tn = _next_tile(b, i, n_b, n_t)

    def step(h_cur, h_nxt):
        tie = _modnorm_rows(h_nxt, 0, xn_ref, g_ref, mod_rows(sc_ref, bn, tn), mod_rows(sh_ref, bn, tn))
        h = h_cur[...]
        n_main = main_ref.shape[1]
        for c in range(n_main // PROJ_TN):
            cols = slice(c * PROJ_TN, (c + 1) * PROJ_TN)
            main_ref[:, cols] = jnp.dot(h, w_ref[:, cols], preferred_element_type=F32).astype(main_ref.dtype)
        tail_ref[...] = jnp.dot(_tied(h, tie), w_ref[:, n_main:], preferred_element_type=F32)

    pl.when(tile % 2 == 0)(functools.partial(step, h_even, h_odd))
    pl.when(tile % 2 == 1)(functools.partial(step, h_odd, h_even))


def modnorm_proj(x, g, sh, sc, w, *, tm, tail_width):
    b, s, d = x.shape
    n = w.shape[1]
    n_t = s // tm
    const = lambda shape: pl.BlockSpec(shape, lambda b, i: (0,) * len(shape))
    out_row = lambda width: pl.BlockSpec((None, tm, width), lambda b, i: (b, i, 0))
    return pl.pallas_call(
        _proj_kernel,
        grid=(b, n_t),
        in_specs=[
            pl.BlockSpec((None, tm, d), lambda b, i: (0, 0, 0)),
            pl.BlockSpec((None, tm, d), lambda bb, i: _next_tile(bb, i, b, n_t) + (0,)),
            const((1, d)), const(sh.shape), const(sc.shape),
            pl.BlockSpec((d, n), lambda b, i: (0, 0), pipeline_mode=pl.Buffered(1)),
        ],
        out_specs=[out_row(n - tail_width), out_row(tail_width)],
        out_shape=[jax.ShapeDtypeStruct((b, s, n - tail_width), BF16),
                   jax.ShapeDtypeStruct((b, s, tail_width), F32)],
        scratch_shapes=[pltpu.VMEM((tm, d), BF16), pltpu.VMEM((tm, d), BF16), pltpu.VMEM((tm, LANES), F32)],
        compiler_params=_params("arbitrary", "arbitrary"),
        name="modnorm_proj",
    )(x, x, g.reshape(1, d), sh, sc, w)


def _kv_variants(t):
    low = lax.broadcasted_iota(jnp.int32, (1, LANES), 1) < HEAD_DIM
    swapped = pltpu.roll(t, HEAD_DIM, 1)
    zero = jnp.zeros_like(t)
    return {
        (0, 0): jnp.where(low, t, zero).astype(BF16), (0, 1): jnp.where(low, zero, swapped).astype(BF16),
        (1, 0): jnp.where(low, swapped, zero).astype(BF16), (1, 1): jnp.where(low, zero, t).astype(BF16),
    }


def _pad_rows(x, rows):
    return jnp.concatenate([x, jnp.zeros((rows - x.shape[0], x.shape[1]), x.dtype)], axis=0)


def _attn_rows(q_rows, k_var, v_var, bias_of):
    r = q_rows.shape[0]
    blocks_per_kv = ATTN_GROUP // 2
    low = lax.broadcasted_iota(jnp.int32, (ATTN_KEY_PAD, LANES), 1) < HEAD_DIM
    ones = (jnp.where(low, 1.0, 0.0).astype(BF16), jnp.where(low, 0.0, 1.0).astype(BF16))
    outs = []
    for h in range(ATTN_KV_HEADS):
        stack = jnp.concatenate(
            [q_rows[:, (h * blocks_per_kv + t) * LANES:(h * blocks_per_kv + t + 1) * LANES]
             for t in range(blocks_per_kv)], axis=0)
        stack = (stack.astype(F32) * (HEAD_DIM ** -0.5)).astype(BF16)
        res = None
        for half in range(2):
            keys = _pad_rows(k_var[h, half], ATTN_KEY_PAD)
            s = lax.dot_general(stack, keys, (((1,), (1,)), ((), ())), preferred_element_type=F32)
            s = s + bias_of(h, half)
            p = jnp.exp(s - jnp.max(s, axis=1, keepdims=True)).astype(BF16)
            rhs = jnp.concatenate([_pad_rows(v_var[h, half], ATTN_KEY_PAD), ones[half]], axis=1)
            part = jnp.dot(p, rhs, preferred_element_type=F32)
            res = part if res is None else res + part
        blk = res[:, :LANES] / res[:, LANES:]
        outs.extend(blk[t * r:(t + 1) * r] for t in range(blocks_per_kv))
    return jnp.concatenate(outs, axis=1)


def _alibi_slopes():
    return 2.0 ** (-8.0 * np.arange(1, ATTN_HEADS + 1) / ATTN_HEADS)


def _attn_bias(dist, valid, sinks):
    q, s = dist.shape
    blocks = ATTN_GROUP // 2
    order = lambda a: np.transpose(a.reshape((ATTN_KV_HEADS, blocks, 2) + a.shape[1:]), (0, 2, 1, 3, 4))
    base = np.where(valid[None], -_alibi_slopes()[:, None, None] * dist[None], -np.inf)
    base = order(base).reshape(ATTN_KV_HEADS, 2, blocks * q, s).astype(np.float32)
    sink = jnp.transpose(sinks.astype(F32).reshape(ATTN_KV_HEADS, blocks, 2), (0, 2, 1))
    sink = jnp.broadcast_to(sink[:, :, :, None, None], (ATTN_KV_HEADS, 2, blocks, q, 1))
    sink = sink.reshape(ATTN_KV_HEADS, 2, blocks * q, 1)
    tail = jnp.full((ATTN_KV_HEADS, 2, blocks * q, ATTN_KEY_PAD - s - 1), -jnp.inf, F32)
    return jnp.concatenate([jnp.asarray(base), sink, tail], axis=-1)


def _attn_prompt_kernel(q_ref, kvc_ref, kvp_ref, bias_ref, o_ref, *, n_chunks):
    i = pl.program_id(1)
    kv = jnp.concatenate([kvp_ref[...], kvc_ref[...]], axis=0)
    k_var = _kv_variants(kv[:, :ATTN_KV_DIM])
    v_var = _kv_variants(kv[:, ATTN_KV_DIM:])
    span = (WINDOW_CHUNKS + 1) * CHUNK
    for cc in range(n_chunks):
        variant = jnp.minimum(i * n_chunks + cc, WINDOW_CHUNKS)
        band = slice(cc * CHUNK, cc * CHUNK + span)
        rows = pl.ds(cc * CHUNK, CHUNK)
        o_ref[rows, :] = _attn_rows(
            q_ref[rows, :], {key: t[band] for key, t in k_var.items()}, {key: t[band] for key, t in v_var.items()},
            lambda h, half: bias_ref[variant, h, half]).astype(o_ref.dtype)


def attn_prompt(pa, pb, sinks, *, qb, out_dtype=BF16):
    b, s, _ = pa.shape
    n_chunks = qb // CHUNK
    span = (WINDOW_CHUNKS + 1) * CHUNK
    qi = np.arange(CHUNK)
    kj = np.arange(span)
    dist = np.abs(qi[:, None] + WINDOW_CHUNKS * CHUNK - kj[None, :]).astype(np.float64)
    bias = jnp.stack([
        _attn_bias(dist, np.broadcast_to(c - WINDOW_CHUNKS + kj[None, :] // CHUNK >= 0, dist.shape), sinks)
        for c in range(WINDOW_CHUNKS + 1)])
    kv_w = 2 * ATTN_KV_DIM
    return pl.pallas_call(
        functools.partial(_attn_prompt_kernel, n_chunks=n_chunks),
        grid=(b, s // qb),
        in_specs=[
            pl.BlockSpec((None, qb, ATTN_Q_DIM), lambda b, i: (b, i, PA_Q // ATTN_Q_DIM)),
            pl.BlockSpec((None, qb, kv_w), lambda b, i: (b, i, PB_K // kv_w)),
            pl.BlockSpec((None, WINDOW, kv_w), lambda b, i: (b, jnp.maximum(i * (qb // WINDOW) - 1, 0), PB_K // kv_w)),
            pl.BlockSpec(bias.shape, lambda b, i: (0,) * bias.ndim),
        ],
        out_specs=pl.BlockSpec((None, qb, ATTN_Q_DIM), lambda b, i: (b, i, 0)),
        out_shape=jax.ShapeDtypeStruct((b, s, ATTN_Q_DIM), out_dtype),
        compiler_params=_params("arbitrary", "arbitrary"),
        name="attn_prompt",
    )(pa, pb, pb, bias)


def _attn_sample_kernel(q_ref, kvn_ref, kc_ref, vc_ref, bias_ref, o_ref):
    kvn = kvn_ref[...]
    k_var = _kv_variants(jnp.concatenate([kc_ref[...], kvn[:, :ATTN_KV_DIM]], axis=0))
    v_var = _kv_variants(jnp.concatenate([vc_ref[...], kvn[:, ATTN_KV_DIM:]], axis=0))
    o_ref[...] = _attn_rows(q_ref[...], k_var, v_var, lambda h, half: bias_ref[h, half]).astype(o_ref.dtype)


def attn_sample(pa, pb, k_cache, v_cache, sinks, *, out_dtype=BF16):
    b, length, _ = pa.shape
    rows = k_cache.shape[1]
    qpos = PAST_LEN + np.arange(length)
    kpos = PAST_LEN - rows + np.arange(rows + length)
    qch, kch = qpos // CHUNK, kpos // CHUNK
    valid = (kch[None, :] <= qch[:, None]) & (kch[None, :] >= qch[:, None] - WINDOW_CHUNKS)
    dist = np.abs(qpos[:, None] - kpos[None, :]).astype(np.float64)
    bias = _attn_bias(dist, valid, sinks)
    kv_w = 2 * ATTN_KV_DIM
    cache = pl.BlockSpec((None, rows, ATTN_KV_DIM), lambda b: (b, 0, 0))
    return pl.pallas_call(
        _attn_sample_kernel,
        grid=(b,),
        in_specs=[
            pl.BlockSpec((None, length, ATTN_Q_DIM), lambda b: (b, 0, PA_Q // ATTN_Q_DIM)),
            pl.BlockSpec((None, length, kv_w), lambda b: (b, 0, PB_K // kv_w)),
            cache, cache,
            pl.BlockSpec(bias.shape, lambda b: (0,) * bias.ndim),
        ],
        out_specs=pl.BlockSpec((None, length, ATTN_Q_DIM), lambda b: (b, 0, 0)),
        out_shape=jax.ShapeDtypeStruct((b, length, ATTN_Q_DIM), out_dtype),
        compiler_params=_params("arbitrary"),
        name="attn_sample",
    )(pa, pb, k_cache, v_cache, bias)


def _causal_conv(pad_ref, prev_ref, cur, w_ref, b_ref, first, q):
    k = w_ref.shape[0]

    @pl.when(first)
    def _():
        pad_ref[pl.ds(0, SUBLANES), :] = prev_ref[...]

    pad_ref[pl.ds(SUBLANES, q), :] = cur
    acc = b_ref[...] + w_ref[k - 1:k, :] * cur
    for t in range(k - 1):
        acc = acc + w_ref[t:t + 1, :] * pad_ref[pl.ds(SUBLANES - (k - 1) + t, q), :]
    pad_ref[pl.ds(0, SUBLANES), :] = pad_ref[pl.ds(q, SUBLANES), :]
    return acc


def _split3(x):
    hi = x.astype(BF16)
    r = x - hi.astype(F32)
    mid = r.astype(BF16)
    lo = (r - mid.astype(F32)).astype(BF16)
    return hi, mid, lo


def _ssd_kernel(x_ref, bc_ref, z_ref, dt_ref, xprev_ref, bcprev_ref, h0_ref,
                wx_ref, bx_ref, wbc_ref, bbc_ref, dtb_ref, alog_ref, dskip_ref, ng_ref,
                y_ref, hout_ref, xpad, bcpad, h_scr, *, q, n_blocks, valid_len):
    i = pl.program_id(1)
    first = i == 0

    @pl.when(first)
    def _():
        h_scr[...] = h0_ref[...]

    xs = _silu(_causal_conv(xpad, xprev_ref, x_ref[...].astype(F32), wx_ref, bx_ref, first, q))
    bc = _silu(_causal_conv(bcpad, bcprev_ref, bc_ref[...].astype(F32), wbc_ref, bbc_ref, first, q))
    xs_b = xs.astype(BF16)

    dt = _softplus(dt_ref[...] + dtb_ref[...])
    if valid_len < q:
        row = lax.broadcasted_iota(jnp.int32, dt.shape, 0)
        dt = jnp.where(row < valid_len, dt, 0.0)
    da = dt * (-jnp.exp(alog_ref[...]))
    r_io = lax.broadcasted_iota(jnp.int32, (q, q), 0)
    c_io = lax.broadcasted_iota(jnp.int32, (q, q), 1)
    causal = c_io <= r_io
    tri = jnp.where(causal, 1.0, 0.0).astype(BF16)
    cum = sum(jnp.dot(tri, part, preferred_element_type=F32) for part in _split3(da))
    cum_t = cum.T
    dt_t = dt.T
    lane = lax.broadcasted_iota(jnp.int32, (1, LANES), 1)
    low_half = lane < SSD_HEAD_DIM

    y_pairs = []
    for g in range(SSD_GROUPS):
        b_g = bc[:, g * SSD_STATE:(g + 1) * SSD_STATE]
        c_g = bc[:, (SSD_GROUPS + g) * SSD_STATE:(SSD_GROUPS + g + 1) * SSD_STATE]
        cb = lax.dot_general(c_g.astype(BF16), b_g.astype(BF16), (((1,), (1,)), ((), ())),
                             preferred_element_type=F32)
        b_t = b_g.T
        for pr in range(g * SSD_HPG // 2, (g + 1) * SSD_HPG // 2):
            x_pair = xs_b[:, pr * LANES:(pr + 1) * LANES]
            h_pair = h_scr[pr]
            h_pair_b = h_pair.astype(BF16)
            ys, states, decays = [], [], []
            for j in (2 * pr, 2 * pr + 1):
                col = jnp.broadcast_to(cum[:, j:j + 1], (q, q))
                row = cum_t[j:j + 1, :]
                decay_in = jnp.exp(jnp.where(causal, col - row, -jnp.inf))
                m = (cb * decay_in * dt_t[j:j + 1, :]).astype(BF16)
                c_scaled = (c_g * jnp.exp(col[:, :SSD_STATE])).astype(BF16)
                ys.append(jnp.dot(m, x_pair, preferred_element_type=F32)
                          + jnp.dot(c_scaled, h_pair_b, preferred_element_type=F32))
                last = row[:, q - 1:q]
                w_row = dt_t[j:j + 1, :] * jnp.exp(last - row)
                states.append(jnp.dot((b_t * w_row).astype(BF16), x_pair, preferred_element_type=F32))
                decays.append(jnp.exp(last))
            h_scr[pr] = (jnp.where(low_half, decays[0], decays[1]) * h_pair
                         + jnp.where(low_half, states[0], states[1]))
            y_pairs.append(jnp.where(low_half, ys[0], ys[1]))
    y = jnp.concatenate(y_pairs, axis=1) + dskip_ref[...] * xs
    y = y * _silu(z_ref[...].astype(F32))
    half = SSD_INNER // SSD_GROUPS
    normed = []
    for g in range(SSD_GROUPS):
        yg = y[:, g * half:(g + 1) * half]
        normed.append(yg * lax.rsqrt(jnp.mean(yg * yg, axis=-1, keepdims=True) + EPS))
    y_ref[...] = (jnp.concatenate(normed, axis=1) * ng_ref[...]).astype(y_ref.dtype)

    @pl.when(i == n_blocks - 1)
    def _():
        hout_ref[...] = h_scr[...]


def ssd_mixer(pa, pb, x_prev, bc_prev, h0, conv_w, conv_b, dt_bias, a_log, d_skip, norm_g, *, q, valid_len=None,
              out_dtype=BF16):
    b, s, _ = pa.shape
    n_blocks = s // q
    valid_len = q if valid_len is None else valid_len
    n_pairs = SSD_HEADS // 2
    pad128 = lambda v: jnp.pad(v.astype(F32), (0, LANES - v.shape[0])).reshape(1, LANES)
    col = lambda width, off: pl.BlockSpec((None, q, width), lambda b, i: (b, i, off // width))
    const = lambda shape: pl.BlockSpec(shape, lambda b, i: (0,) * len(shape))
    per_b = lambda shape: pl.BlockSpec((None,) + shape, lambda b, i: (b,) + (0,) * len(shape))
    h_shape = (n_pairs, SSD_STATE, LANES)
    return pl.pallas_call(
        functools.partial(_ssd_kernel, q=q, n_blocks=n_blocks, valid_len=valid_len),
        grid=(b, n_blocks),
        in_specs=[
            col(SSD_INNER, PA_X), col(SSD_BC_DIM, PA_BC), col(SSD_INNER, PA_Z), col(LANES, PB_DT),
            per_b((SUBLANES, SSD_INNER)), per_b((SUBLANES, SSD_BC_DIM)), per_b(h_shape),
            const((SSD_CONV, SSD_INNER)), const((1, SSD_INNER)),
            const((SSD_CONV, SSD_BC_DIM)), const((1, SSD_BC_DIM)),
            const((1, LANES)), const((1, LANES)), const((1, SSD_INNER)), const((1, SSD_INNER)),
        ],
        out_specs=[
            pl.BlockSpec((None, q, SSD_INNER), lambda b, i: (b, i, 0)),
            per_b(h_shape),
        ],
        out_shape=[
            jax.ShapeDtypeStruct((b, s, SSD_INNER), out_dtype),
            jax.ShapeDtypeStruct((b,) + h_shape, F32),
        ],
        scratch_shapes=[
            pltpu.VMEM((q + SUBLANES, SSD_INNER), F32),
            pltpu.VMEM((q + SUBLANES, SSD_BC_DIM), F32),
            pltpu.VMEM(h_shape, F32),
        ],
        compiler_params=_params("arbitrary", "arbitrary"),
        name="ssd_mixer",
    )(pa, pa, pa, pb, x_prev, bc_prev, h0,
      conv_w[:, :SSD_INNER], conv_b[:SSD_INNER].reshape(1, -1),
      conv_w[:, SSD_INNER:], conv_b[SSD_INNER:].reshape(1, -1),
      pad128(dt_bias), pad128(a_log),
      jnp.repeat(d_skip.astype(F32), SSD_HEAD_DIM).reshape(1, -1), norm_g.astype(F32).reshape(1, -1))


def _pair_state(h):
    b = h.shape[0]
    h = h.astype(F32).reshape(b, SSD_HEADS // 2, 2, SSD_HEAD_DIM, SSD_STATE)
    return h.transpose(0, 1, 4, 2, 3).reshape(b, SSD_HEADS // 2, SSD_STATE, 2 * SSD_HEAD_DIM)


def _unpair_state(h):
    b = h.shape[0]
    h = h.reshape(b, SSD_HEADS // 2, SSD_STATE, 2, SSD_HEAD_DIM)
    return h.transpose(0, 1, 3, 4, 2).reshape(b, SSD_HEADS, SSD_HEAD_DIM, SSD_STATE)


def _tail_rows(prev):
    return jnp.pad(prev.astype(F32), ((0, 0), (SUBLANES - prev.shape[1], 0), (0, 0)))


def _outproj_kernel(x_ref, a_ref, y_ref, gt_ref, wa_ref, wy_ref, o_ref):
    m = (jnp.dot(a_ref[...].astype(BF16), wa_ref[...], preferred_element_type=F32)
         + jnp.dot(y_ref[...].astype(BF16), wy_ref[...], preferred_element_type=F32))
    o_ref[...] = x_ref[...] + gt_ref[...] * m


def mix0_out(x, attn, y, gt, w_out, *, tm):
    b, s, d = x.shape
    da, dy = attn.shape[2], y.shape[2]
    mod_rows = 1 if gt.shape[1] == 1 else tm
    x_spec = pl.BlockSpec((None, tm, d), lambda b, i: (b, i, 0))
    return pl.pallas_call(
        _outproj_kernel,
        grid=(b, s // tm),
        in_specs=[
            x_spec,
            pl.BlockSpec((None, tm, da), lambda b, i: (b, i, 0)),
            pl.BlockSpec((None, tm, dy), lambda b, i: (b, i, 0)),
            _mod_spec(mod_rows, d),
            pl.BlockSpec((da, d), lambda b, i: (0, 0)),
            pl.BlockSpec((dy, d), lambda b, i: (1, 0)),
        ],
        out_specs=x_spec,
        out_shape=jax.ShapeDtypeStruct(x.shape, F32),
        compiler_params=_params("arbitrary", "arbitrary"),
        name="mix0_out",
    )(x, attn, y, gt, w_out, w_out)


SCONV_TC = 512


def _sconv_kernel(x_ref, xn_ref, g_ref, sh_ref, sc_ref, gt_ref, prev_ref, cw_ref, win_ref, wout_ref,
                  o_ref, buf_ref, h_even, h_odd, rs_scr, vpad, carry, *, tm):
    b, i = pl.program_id(0), pl.program_id(1)
    n_b, n_t = pl.num_programs(0), pl.num_programs(1)
    tile = b * n_t + i
    bn, _ = _next_tile(b, i, n_b, n_t)
    d = x_ref.shape[1]
    k = cw_ref.shape[0]
    tc = SCONV_TC
    n_c = d // tc

    @pl.when(i == 0)
    def _():
        carry[...] = prev_ref[...]

    @pl.when(tile == 0)
    def _():
        _modnorm_store(h_even, x_ref, g_ref, sc_ref.at[b], sh_ref.at[b], rs_scr)

    def step(h_cur, h_nxt):
        tie = _modnorm_rows(h_nxt, 0, xn_ref, g_ref, sc_ref.at[bn], sh_ref.at[bn])
        h = h_cur[...]

        def in_proj(c, lhs=(h, h, h)):
            return [jnp.dot(lhs[part], win_ref[:, part * d + c * tc:part * d + (c + 1) * tc],
                            preferred_element_type=F32) for part in range(3)]

        def mix(c, gate_b, gate_c, xi):
            cols = slice(c * tc, (c + 1) * tc)
            v = gate_c * xi
            vpad[pl.ds(0, SUBLANES), cols] = carry[:, cols]
            vpad[pl.ds(SUBLANES, tm), cols] = v
            u = cw_ref[k - 1:k, cols] * v
            for t in range(k - 1):
                u = u + cw_ref[t:t + 1, cols] * vpad[pl.ds(SUBLANES - (k - 1) + t, tm), cols]
            carry[:, cols] = vpad[pl.ds(tm, SUBLANES), cols]
            return gate_b * u

        acc = None
        pending = in_proj(0)
        for c in range(n_c):
            r = mix(c, *pending)
            if c == 0:
                r = _tied(r, tie)
            if c + 1 < n_c:
                pending = in_proj(c + 1, (h, h, _tied(h, _zero_tie(r))))
            part = jnp.dot(r.astype(BF16), wout_ref[c * tc:(c + 1) * tc, :], preferred_element_type=F32)
            acc = part if acc is None else acc + part
        o_ref[...] = x_ref[...] + gt_ref[b] * acc
        buf_ref[...] = carry[...]

    pl.when(tile % 2 == 0)(functools.partial(step, h_even, h_odd))
    pl.when(tile % 2 == 1)(functools.partial(step, h_odd, h_even))


def sconv_mixer(x, g, sh, sc, gt, prev, conv_w, w_in, w_out, *, tm):
    b, s, d = x.shape
    n_t = s // tm
    x_spec = pl.BlockSpec((None, tm, d), lambda b, i: (b, i, 0))
    next_spec = pl.BlockSpec((None, tm, d), lambda bb, i: _next_tile(bb, i, b, n_t) + (0,))
    buf_spec = pl.BlockSpec((None, SUBLANES, d), lambda b, i: (b, 0, 0))
    const = lambda shape: pl.BlockSpec(shape, lambda b, i: (0,) * len(shape))
    resident = lambda shape: pl.BlockSpec(shape, lambda b, i: (0, 0), pipeline_mode=pl.Buffered(1))
    return pl.pallas_call(
        functools.partial(_sconv_kernel, tm=tm),
        grid=(b, n_t),
        in_specs=[
            x_spec, next_spec, const((1, d)), const(sh.shape), const(sc.shape), const(gt.shape), buf_spec,
            const((SCONV_WIDTH, d)), resident((d, 3 * d)), resident((d, d)),
        ],
        out_specs=[x_spec, buf_spec],
        out_shape=[jax.ShapeDtypeStruct(x.shape, F32), jax.ShapeDtypeStruct((b, SUBLANES, d), F32)],
        scratch_shapes=[
            pltpu.VMEM((tm, d), BF16), pltpu.VMEM((tm, d), BF16), pltpu.VMEM((tm, LANES), F32),
            pltpu.VMEM((tm + SUBLANES, d), F32), pltpu.VMEM((SUBLANES, d), F32),
        ],
        compiler_params=_params("arbitrary", "arbitrary"),
        name="sconv_mixer",
    )(x, x, g.reshape(1, d), sh, sc, gt, prev, conv_w.astype(F32), w_in, w_out)


def _arrange_w_in0(w):
    i1 = ATTN_Q_DIM
    i2 = i1 + ATTN_KV_DIM
    i3 = i2 + ATTN_KV_DIM
    i4 = i3 + SSD_INNER
    i5 = i4 + SSD_CONV_DIM
    q, k, v, z, xbc, dt = (w[:, a:b] for a, b in ((0, i1), (i1, i2), (i2, i3), (i3, i4), (i4, i5), (i5, w.shape[1])))
    pad = jnp.zeros((w.shape[0], PB_WIDTH - PB_DT - dt.shape[1]), w.dtype)
    return jnp.concatenate([q, z, xbc, k, v, dt, pad], axis=1).astype(BF16)


def _split_mod(mod):
    return [m[:, None, :] for m in jnp.split(mod, N_MOD, axis=-1)]


def _per_token(m, length):
    b, _, d = m.shape
    return jnp.broadcast_to(m, (b, length, d)).reshape(1, b * length, d)


class _Tiles(NamedTuple):
    ffn: int
    proj: int
    mix_out: int
    sconv: int
    ssd: int
    attn: int


def _tiles(batch, seq, flat):
    if flat:
        rows = batch * seq
        return _Tiles(ffn=rows, proj=rows, mix_out=rows, sconv=seq, ssd=LANES, attn=seq)
    return _Tiles(ffn=min(1024, seq), proj=min(512, seq), mix_out=min(512, seq), sconv=min(256, seq),
                  ssd=min(256, seq), attn=min(256, seq))


def _trunk(x, mods, weights, cache):
    b, s, d = x.shape
    t = _tiles(b, s, cache is not None)
    tm, tm_proj, tm_mix, q_ssd, qb_attn = t.ffn, t.proj, t.mix_out, t.ssd, t.attn
    flat = cache is not None
    if flat:
        as_rows = lambda t: t.reshape(1, b * s, t.shape[-1])
        mod_of = lambda m: _per_token(m, s)
    else:
        as_rows = lambda t: t
        mod_of = lambda m: m
    unrow = lambda t: t.reshape(b, s, t.shape[-1])
    w = weights

    def ffn(xr, l, k, sh, sc, gt, final_g=None):
        return ffn_half(xr, w["norm_g"][l, 2 * k], mod_of(sh), mod_of(sc), mod_of(gt),
                        w["ffn_gate"], w["ffn_up"], w["ffn_down"], l, k, final_g, tm=tm)

    sh1, sc1, g1, sh2, sc2, g2, sh3, sc3, g3 = mods[0]
    xr = ffn(as_rows(x), 0, 0, sh1, sc1, g1)
    pa, pb = modnorm_proj(xr, w["norm_g"][0, 1], mod_of(sh2), mod_of(sc2), w["w_in0"], tm=tm_proj,
                          tail_width=PB_WIDTH)
    pa, pb = unrow(pa), unrow(pb)
    if cache is None:
        attn = attn_prompt(pa, pb, w["sinks"], qb=qb_attn)
        x_prev = jnp.zeros((b, SUBLANES, SSD_INNER), F32)
        bc_prev = jnp.zeros((b, SUBLANES, SSD_BC_DIM), F32)
        h0 = jnp.zeros((b, SSD_HEADS // 2, SSD_STATE, LANES), F32)
        y, h_new = ssd_mixer(pa, pb, x_prev, bc_prev, h0, *w["ssd"], q=q_ssd)
        new_kv = pb[:, s - WINDOW:]
        sconv_prev = jnp.zeros((b, SUBLANES, d), F32)
    else:
        k_cache, v_cache, h_state, conv_prev, sconv_state = cache
        rows = k_cache.shape[1]
        attn = attn_sample(pa, pb, k_cache.reshape(b, rows, ATTN_KV_DIM), v_cache.reshape(b, rows, ATTN_KV_DIM),
                           w["sinks"])
        tail = _tail_rows(conv_prev)
        pad_rows = lambda t: jnp.pad(t, ((0, 0), (0, q_ssd - s), (0, 0)))
        y, h_new = ssd_mixer(pad_rows(pa), pad_rows(pb), tail[:, :, :SSD_INNER], tail[:, :, SSD_INNER:],
                             _pair_state(h_state), *w["ssd"], q=q_ssd, valid_len=s)
        y = y[:, :s]
        new_kv = pb
        sconv_prev = _tail_rows(sconv_state)
    xbc_raw = pa[:, s - (SSD_CONV - 1):, PA_X:PA_X + SSD_CONV_DIM].astype(F32)
    kv_shape = (b, new_kv.shape[1], ATTN_KV_HEADS, HEAD_DIM)
    states0 = (new_kv[:, :, PB_K:PB_K + ATTN_KV_DIM].reshape(kv_shape),
               new_kv[:, :, PB_V:PB_V + ATTN_KV_DIM].reshape(kv_shape), _unpair_state(h_new), xbc_raw)
    xr = mix0_out(xr, as_rows(attn), as_rows(y), mod_of(g2), w["w_out0"], tm=tm_mix)
    xr = ffn(xr, 0, 1, sh3, sc3, g3)

    sh1, sc1, g1, sh2, sc2, g2, sh3, sc3, g3 = mods[1]
    xr = ffn(xr, 1, 0, sh1, sc1, g1)
    xm, buf = sconv_mixer(unrow(xr), w["norm_g"][1, 1], sh2, sc2, g2, sconv_prev, w["sconv_w"], w["w_in1"],
                          w["w_out1"], tm=t.sconv)
    sconv_new = buf[:, SUBLANES - (SCONV_WIDTH - 1):]
    y_out = unrow(ffn(as_rows(xm), 1, 1, sh3, sc3, g3, final_g=w["final_g"]))
    return y_out, states0, sconv_new


def kernel(x_prompt, x_sample, c_prompt, c_sample, cache_swa_k, cache_swa_v, state_ssd, state_ssd_conv, state_sconv, norm_g, w_ada, b_ada, w_ffn_gate, w_ffn_up, w_ffn_down, w_in_mix0, w_out_mix0, attn_sinks, ssd_conv_w, ssd_conv_b, ssd_dt_bias, ssd_a_log, ssd_d, ssd_norm_g, w_in_mix1, sconv_w, w_out_mix1, final_norm_g):
    bp, bs = c_prompt.shape[0], c_sample.shape[0]
    c_all = jnp.concatenate([c_prompt, c_sample], axis=0)
    c_rows = -(-c_all.shape[0] // SUBLANES) * SUBLANES
    c_all = jnp.pad(c_all, ((0, c_rows - c_all.shape[0]), (0, 0)))
    mod = ada_modulation(c_all, w_ada, b_ada)
    mods_p = [_split_mod(mod[l, :bp]) for l in range(mod.shape[0])]
    mods_s = [_split_mod(mod[l, bp:bp + bs]) for l in range(mod.shape[0])]

    weights = {
        "norm_g": norm_g,
        "ffn_gate": _chunk_major(w_ffn_gate), "ffn_up": _chunk_major(w_ffn_up), "ffn_down": w_ffn_down.astype(BF16),
        "w_in0": _arrange_w_in0(w_in_mix0[0]), "w_out0": w_out_mix0[0].astype(BF16),
        "sinks": attn_sinks[0],
        "ssd": (ssd_conv_w[0].astype(F32), ssd_conv_b[0].astype(F32), ssd_dt_bias[0], ssd_a_log[0], ssd_d[0],
                ssd_norm_g[0]),
        "w_in1": w_in_mix1[0].astype(BF16), "sconv_w": sconv_w[0], "w_out1": w_out_mix1[0].astype(BF16),
        "final_g": final_norm_g,
    }
    y_p, (k_p, v_p, h_p, cv_p), sc_p = _trunk(x_prompt, mods_p, weights, None)
    cache = (cache_swa_k[0], cache_swa_v[0], state_ssd[0], state_ssd_conv[0], state_sconv[0])
    y_s, (k_s, v_s, h_s, cv_s), sc_s = _trunk(x_sample, mods_s, weights, cache)
    stack = lambda t: t[None]
    return (y_p, y_s, stack(k_p), stack(v_p), stack(h_p), stack(cv_p), stack(sc_p),
            stack(k_s), stack(v_s), stack(h_s), stack(cv_s), stack(sc_s))
```

```python
import functools
from typing import Any, NamedTuple

import numpy as np
import jax
import jax.numpy as jnp
from jax import lax
from jax.experimental import pallas as pl
from jax.experimental.pallas import tpu as pltpu

F32 = jnp.float32
BF16 = jnp.bfloat16

EPS = 1e-6
CHUNK = 64
PAST_LEN = 2048
ATTN_HEADS = 16
ATTN_KV_HEADS = 2
ATTN_GROUP = ATTN_HEADS // ATTN_KV_HEADS
HEAD_DIM = 64
WINDOW = 128
WINDOW_CHUNKS = WINDOW // CHUNK
ATTN_Q_DIM = ATTN_HEADS * HEAD_DIM
ATTN_KV_DIM = ATTN_KV_HEADS * HEAD_DIM
SSD_HEADS = 16
SSD_HEAD_DIM = 64
SSD_INNER = SSD_HEADS * SSD_HEAD_DIM
SSD_STATE = 128
SSD_GROUPS = 2
SSD_HPG = SSD_HEADS // SSD_GROUPS
SSD_CONV = 4
SSD_BC_DIM = 2 * SSD_GROUPS * SSD_STATE
SSD_CONV_DIM = SSD_INNER + SSD_BC_DIM
SCONV_WIDTH = 3
N_MOD = 9

LANES = 128
SUBLANES = 8
MXU_DIM = 256
VMEM_LIMIT = 62 * 1024 * 1024

PROJ_TN = 512
PA_Q = 0
PA_Z = PA_Q + ATTN_Q_DIM
PA_X = PA_Z + SSD_INNER
PA_BC = PA_X + SSD_INNER
PA_WIDTH = PA_BC + SSD_BC_DIM
PB_K = 0
PB_V = PB_K + ATTN_KV_DIM
PB_DT = PB_V + ATTN_KV_DIM
PB_WIDTH = PROJ_TN
ATTN_KEY_PAD = MXU_DIM
assert HEAD_DIM * 2 == LANES and ATTN_KV_DIM == LANES and PA_WIDTH % PROJ_TN == 0


def _params(*sem):
    return pltpu.CompilerParams(dimension_semantics=sem, vmem_limit_bytes=VMEM_LIMIT)


def _silu(x):
    half = 0.5 * x
    return half + half * jnp.tanh(half)


def _softplus(x):
    return jnp.maximum(x, 0.0) + jnp.log(1.0 + jnp.exp(-jnp.abs(x)))


NORM_ROWS = 2 * SUBLANES


def _modnorm_store(h_ref, x_ref, g_ref, sc_ref, sh_ref, rs_scr):
    tm, d = x_ref.shape
    for c in range(tm // SUBLANES):
        r = pl.ds(c * SUBLANES, SUBLANES)
        x = x_ref[r, :]
        ms = jnp.mean(x * x, axis=-1, keepdims=True)
        rs_scr[r, :] = jnp.broadcast_to(lax.rsqrt(ms + EPS), (SUBLANES, LANES))
    per_row = sc_ref.shape[0] != 1
    scale = None if per_row else g_ref[...] * (1.0 + sc_ref[...])

    def body(c, carry):
        r = pl.ds(pl.multiple_of(c * NORM_ROWS, NORM_ROWS), NORM_ROWS)
        rs = jnp.concatenate([rs_scr[r, :]] * (d // LANES), axis=1)
        y = x_ref[r, :] * rs
        if per_row:
            h = y * (g_ref[...] * (1.0 + sc_ref[r, :])) + sh_ref[r, :]
        else:
            h = y * scale + sh_ref[...]
        h_ref[r, :] = h.astype(BF16)
        return carry

    lax.fori_loop(0, tm // NORM_ROWS, body, 0, unroll=2 if tm >= 2 * NORM_ROWS else 1)


def _mod_spec(mod_rows, d):
    if mod_rows == 1:
        return pl.BlockSpec((None, 1, d), lambda b, i, *_: (b, 0, 0))
    return pl.BlockSpec((None, mod_rows, d), lambda b, i, *_: (b, i, 0))


def _ada_kernel(c_ref, w_ref, b_ref, o_ref):
    c = c_ref[...]
    s = _silu(c).astype(BF16)
    o_ref[...] = jnp.dot(s, w_ref[...].astype(BF16), preferred_element_type=F32) + b_ref[...]


def ada_modulation(c, w_ada, b_ada, tn=1024):
    n_layers, d, n = w_ada.shape
    r = c.shape[0]
    return pl.pallas_call(
        _ada_kernel,
        grid=(n_layers, n // tn),
        in_specs=[
            pl.BlockSpec((r, d), lambda l, j: (0, 0)),
            pl.BlockSpec((None, d, tn), lambda l, j: (l, 0, j)),
            pl.BlockSpec((None, 1, tn), lambda l, j: (l, 0, j)),
        ],
        out_specs=pl.BlockSpec((None, r, tn), lambda l, j: (l, 0, j)),
        out_shape=jax.ShapeDtypeStruct((n_layers, r, n), F32),
        compiler_params=_params("arbitrary", "arbitrary"),
        name="ada_modulation",
    )(c, w_ada, b_ada.reshape(n_layers, 1, n))


def _rmsnorm_inplace(o_ref, rs_scr, fg_ref):
    tm, d = o_ref.shape
    n_steps = tm // NORM_ROWS
    for c in range(tm // SUBLANES):
        r = pl.ds(c * SUBLANES, SUBLANES)
        xn = o_ref[r, :]
        ms = jnp.mean(xn * xn, axis=-1, keepdims=True)
        rs_scr[r, :] = jnp.broadcast_to(lax.rsqrt(ms + EPS), (SUBLANES, LANES))

    def scale(c, carry):
        r = pl.ds(pl.multiple_of(c * NORM_ROWS, NORM_ROWS), NORM_ROWS)
        rs = jnp.concatenate([rs_scr[r, :]] * (d // LANES), axis=1)
        o_ref[r, :] = o_ref[r, :] * rs * fg_ref[...]
        return carry

    lax.fori_loop(0, n_steps, scale, 0, unroll=2 if n_steps >= 2 else 1)


NORM_SPLIT = 8


def _modnorm_rows(h_ref, row0, x_ref, g_ref, sc_ref, sh_ref):
    rows = x_ref.shape[0]
    per_row = sc_ref.shape[0] != 1
    if not per_row:
        scale = g_ref[...] * (1.0 + sc_ref[...])
    tie = None
    for c in range(rows // NORM_ROWS):
        r = pl.ds(c * NORM_ROWS, NORM_ROWS)
        x = x_ref[r, :]
        y = x * lax.rsqrt(jnp.mean(x * x, axis=-1, keepdims=True) + EPS)
        if per_row:
            hv = y * (g_ref[...] * (1.0 + sc_ref[r, :])) + sh_ref[r, :]
        else:
            hv = y * scale + sh_ref[...]
        h_ref[pl.ds(row0 + c * NORM_ROWS, NORM_ROWS), :] = hv.astype(BF16)
        zero = _zero_tie(hv)
        tie = zero if tie is None else tie | zero
    return tie


def _zero_tie(v):
    bits = pltpu.bitcast(v[:NORM_ROWS, :LANES], jnp.uint32)
    return (bits >> 16) >> 16


def _tied(v, tie):
    r, c = tie.shape
    head = v[:r, :c] + tie.astype(jnp.int32).astype(F32).astype(v.dtype)
    top = jnp.concatenate([head, v[:r, c:]], axis=1)
    return top if v.shape[0] == r else jnp.concatenate([top, v[r:]], axis=0)


def _next_tile(b, i, n_b, n_t):
    nxt = jnp.minimum(b * n_t + i + 1, n_b * n_t - 1)
    return nxt // n_t, nxt % n_t


def _ffn_kernel(x_ref, xn_ref, g_ref, sh_ref, sc_ref, gt_ref, wg_ref, wu_ref, wd_ref, *rest, n_ff, final):
    if final:
        fg_ref, o_ref, h_even, h_odd, rs_scr = rest
    else:
        fg_ref = None
        o_ref, h_even, h_odd, rs_scr = rest
    b, i, j = pl.program_id(0), pl.program_id(1), pl.program_id(2)
    n_b, n_t = pl.num_programs(0), pl.num_programs(1)
    tile = b * n_t + i
    tm, rows_n = x_ref.shape[0], xn_ref.shape[0]
    per_row = sc_ref.shape[1] != 1

    def mod_rows(ref, bb, row0, rows):
        return ref.at[bb, pl.ds(row0, rows)] if per_row else ref.at[bb]

    cur = functools.partial(mod_rows, bb=b, row0=pl.multiple_of(i * tm, tm), rows=tm)

    @pl.when((tile == 0) & (j == 0))
    def _():
        _modnorm_store(h_even, x_ref, g_ref, cur(sc_ref), cur(sh_ref), rs_scr)

    bn, tn = _next_tile(b, i, n_b, n_t)
    part = jnp.minimum(j, NORM_SPLIT - 1)
    nxt = functools.partial(mod_rows, bb=bn, row0=pl.multiple_of(tn * tm + part * rows_n, rows_n), rows=rows_n)

    def step(h_cur, h_nxt, last):
        tie = _modnorm_rows(h_nxt, pl.multiple_of(part * rows_n, rows_n), xn_ref, g_ref, nxt(sc_ref), nxt(sh_ref))
        h = h_cur[...]
        half = wg_ref.shape[1] // 2
        acts = []
        for c in range(2):
            cols = slice(c * half, (c + 1) * half)
            g = jnp.dot(h, wg_ref[:, cols], preferred_element_type=F32)
            u = jnp.dot(h, wu_ref[:, cols], preferred_element_type=F32)
            act = _silu(g) * u
            if c == 0:
                act = _tied(act, tie)
            acts.append(act.astype(BF16))
        total = jnp.where(j == 0, 0.0, o_ref[...])
        for c in range(2):
            total = total + jnp.dot(acts[c], wd_ref[c * half:(c + 1) * half, :], preferred_element_type=F32)
        if last:
            total = x_ref[...] + 0.5 * cur(gt_ref)[...] * total
        o_ref[...] = total

    for parity, (h_cur, h_nxt) in enumerate(((h_even, h_odd), (h_odd, h_even))):
        pl.when((tile % 2 == parity) & (j < n_ff - 1))(functools.partial(step, h_cur, h_nxt, False))
        pl.when((tile % 2 == parity) & (j == n_ff - 1))(functools.partial(step, h_cur, h_nxt, True))

    if final:
        @pl.when(j == n_ff - 1)
        def _():
            _rmsnorm_inplace(o_ref, rs_scr, fg_ref)


FFN_TF = 512


def ffn_half(x, g, sh, sc, gt, wg, wu, wd, layer, half, final_g=None, *, tm, tf=FFN_TF):
    b, s, d = x.shape
    n_ff = wg.shape[-1] // tf
    n_t = s // tm
    assert n_ff >= NORM_SPLIT and tm % (NORM_SPLIT * NORM_ROWS) == 0
    rows_n = tm // NORM_SPLIT
    final = final_g is not None
    x_spec = pl.BlockSpec((None, tm, d), lambda b, i, j: (b, i, 0))

    def next_rows(bb, i, j):
        bn, tn = _next_tile(bb, i, b, n_t)
        return bn, tn * NORM_SPLIT + jnp.minimum(j, NORM_SPLIT - 1), 0

    whole = pl.BlockSpec(sh.shape, lambda b, i, j: (0, 0, 0))
    in_specs = [
        x_spec,
        pl.BlockSpec((None, rows_n, d), next_rows),
        pl.BlockSpec((1, d), lambda b, i, j: (0, 0)),
        whole, whole, whole,
        pl.BlockSpec((None, None, d, tf), lambda b, i, j: (layer, half, 0, j)),
        pl.BlockSpec((None, None, d, tf), lambda b, i, j: (layer, half, 0, j)),
        pl.BlockSpec((None, None, tf, d), lambda b, i, j: (layer, half, j, 0)),
    ]
    args = [x, x, g.reshape(1, d), sh, sc, gt, wg, wu, wd]
    if final:
        in_specs.append(pl.BlockSpec((1, d), lambda b, i, j: (0, 0)))
        args.append(final_g.reshape(1, d))
    return pl.pallas_call(
        functools.partial(_ffn_kernel, n_ff=n_ff, final=final),
        grid=(b, n_t, n_ff),
        in_specs=in_specs,
        out_specs=x_spec,
        out_shape=jax.ShapeDtypeStruct(x.shape, F32),
        scratch_shapes=[pltpu.VMEM((tm, d), BF16), pltpu.VMEM((tm, d), BF16), pltpu.VMEM((tm, LANES), F32)],
        compiler_params=_params("arbitrary", "arbitrary", "arbitrary"),
        name="ffn_half_final" if final else "ffn_half",
    )(*args)


def _proj_kernel(x0_ref, xn_ref, g_ref, sh_ref, sc_ref, w_ref, main_ref, tail_ref, h_even, h_odd, rs_scr):
    b, i = pl.program_id(0), pl.program_id(1)
    n_b, n_t = pl.num_programs(0), pl.num_programs(1)
    tile = b * n_t + i
    tm = xn_ref.shape[0]
    per_row = sc_ref.shape[1] != 1

    def mod_rows(ref, bb, tt):
        return ref.at[bb, pl.ds(pl.multiple_of(tt * tm, tm), tm)] if per_row else ref.at[bb]

    @pl.when(tile == 0)
    def _():
        _modnorm_store(h_even, x0_ref, g_ref, mod_rows(sc_ref, 0, 0), mod_rows(sh_ref, 0, 0), rs_scr)

    bn, tn = _next_tile(b, i, n_b, n_t)

    def step(h_cur, h_nxt):
        tie = _modnorm_rows(h_nxt, 0, xn_ref, g_ref, mod_rows(sc_ref, bn, tn), mod_rows(sh_ref, bn, tn))
        h = h_cur[...]
        n_main = main_ref.shape[1]
        for c in range(n_main // PROJ_TN):
            cols = slice(c * PROJ_TN, (c + 1) * PROJ_TN)
            main_ref[:, cols] = jnp.dot(h, w_ref[:, cols], preferred_element_type=F32).astype(main_ref.dtype)
        tail_ref[...] = jnp.dot(_tied(h, tie), w_ref[:, n_main:], preferred_element_type=F32)

    pl.when(tile % 2 == 0)(functools.partial(step, h_even, h_odd))
    pl.when(tile % 2 == 1)(functools.partial(step, h_odd, h_even))


def modnorm_proj(x, g, sh, sc, w, *, tm, tail_width):
    b, s, d = x.shape
    n = w.shape[1]
    n_t = s // tm
    const = lambda shape: pl.BlockSpec(shape, lambda b, i: (0,) * len(shape))
    out_row = lambda width: pl.BlockSpec((None, tm, width), lambda b, i: (b, i, 0))
    return pl.pallas_call(
        _proj_kernel,
        grid=(b, n_t),
        in_specs=[
            pl.BlockSpec((None, tm, d), lambda b, i: (0, 0, 0)),
            pl.BlockSpec((None, tm, d), lambda bb, i: _next_tile(bb, i, b, n_t) + (0,)),
            const((1, d)), const(sh.shape), const(sc.shape),
            pl.BlockSpec((d, n), lambda b, i: (0, 0), pipeline_mode=pl.Buffered(1)),
        ],
        out_specs=[out_row(n - tail_width), out_row(tail_width)],
        out_shape=[jax.ShapeDtypeStruct((b, s, n - tail_width), BF16),
                   jax.ShapeDtypeStruct((b, s, tail_width), F32)],
        scratch_shapes=[pltpu.VMEM((tm, d), BF16), pltpu.VMEM((tm, d), BF16), pltpu.VMEM((tm, LANES), F32)],
        compiler_params=_params("arbitrary", "arbitrary"),
        name="modnorm_proj",
    )(x, x, g.reshape(1, d), sh, sc, w)


def _kv_variants(t):
    low = lax.broadcasted_iota(jnp.int32, (1, LANES), 1) < HEAD_DIM
    swapped = pltpu.roll(t, HEAD_DIM, 1)
    zero = jnp.zeros_like(t)
    return {
        (0, 0): jnp.where(low, t, zero).astype(BF16), (0, 1): jnp.where(low, zero, swapped).astype(BF16),
        (1, 0): jnp.where(low, swapped, zero).astype(BF16), (1, 1): jnp.where(low, zero, t).astype(BF16),
    }


def _pad_rows(x, rows):
    return jnp.concatenate([x, jnp.zeros((rows - x.shape[0], x.shape[1]), x.dtype)], axis=0)


def _attn_rows(q_rows, k_var, v_var, bias_of):
    r = q_rows.shape[0]
    blocks_per_kv = ATTN_GROUP // 2
    low = lax.broadcasted_iota(jnp.int32, (ATTN_KEY_PAD, LANES), 1) < HEAD_DIM
    ones = (jnp.where(low, 1.0, 0.0).astype(BF16), jnp.where(low, 0.0, 1.0).astype(BF16))
    outs = []
    for h in range(ATTN_KV_HEADS):
        stack = jnp.concatenate(
            [q_rows[:, (h * blocks_per_kv + t) * LANES:(h * blocks_per_kv + t + 1) * LANES]
             for t in range(blocks_per_kv)], axis=0)
        stack = (stack.astype(F32) * (HEAD_DIM ** -0.5)).astype(BF16)
        res = None
        for half in range(2):
            keys = _pad_rows(k_var[h, half], ATTN_KEY_PAD)
            s = lax.dot_general(stack, keys, (((1,), (1,)), ((), ())), preferred_element_type=F32)
            s = s + bias_of(h, half)
            p = jnp.exp(s - jnp.max(s, axis=1, keepdims=True)).astype(BF16)
            rhs = jnp.concatenate([_pad_rows(v_var[h, half], ATTN_KEY_PAD), ones[half]], axis=1)
            part = jnp.dot(p, rhs, preferred_element_type=F32)
            res = part if res is None else res + part
        blk = res[:, :LANES] / res[:, LANES:]
        outs.extend(blk[t * r:(t + 1) * r] for t in range(blocks_per_kv))
    return jnp.concatenate(outs, axis=1)


def _alibi_slopes():
    return 2.0 ** (-8.0 * np.arange(1, ATTN_HEADS + 1) / ATTN_HEADS)


def _attn_bias(dist, valid, sinks):
    q, s = dist.shape
    blocks = ATTN_GROUP // 2
    order = lambda a: np.transpose(a.reshape((ATTN_KV_HEADS, blocks, 2) + a.shape[1:]), (0, 2, 1, 3, 4))
    base = np.where(valid[None], -_alibi_slopes()[:, None, None] * dist[None], -np.inf)
    base = order(base).reshape(ATTN_KV_HEADS, 2, blocks * q, s).astype(np.float32)
    sink = jnp.transpose(sinks.astype(F32).reshape(ATTN_KV_HEADS, blocks, 2), (0, 2, 1))
    sink = jnp.broadcast_to(sink[:, :, :, None, None], (ATTN_KV_HEADS, 2, blocks, q, 1))
    sink = sink.reshape(ATTN_KV_HEADS, 2, blocks * q, 1)
    tail = jnp.full((ATTN_KV_HEADS, 2, blocks * q, ATTN_KEY_PAD - s - 1), -jnp.inf, F32)
    return jnp.concatenate([jnp.asarray(base), sink, tail], axis=-1)


def _attn_block(q_ref, kvc_ref, kvp_ref, bias_ref, n_chunks):
    i = pl.program_id(1)
    kv = jnp.concatenate([kvp_ref[...], kvc_ref[...]], axis=0)
    k_var = _kv_variants(kv[:, :ATTN_KV_DIM])
    v_var = _kv_variants(kv[:, ATTN_KV_DIM:])
    span = (WINDOW_CHUNKS + 1) * CHUNK
    outs = []
    for cc in range(n_chunks):
        variant = jnp.minimum(i * n_chunks + cc, WINDOW_CHUNKS)
        band = slice(cc * CHUNK, cc * CHUNK + span)
        outs.append(_attn_rows(
            q_ref[pl.ds(cc * CHUNK, CHUNK), :], {key: t[band] for key, t in k_var.items()},
            {key: t[band] for key, t in v_var.items()}, lambda h, half: bias_ref[variant, h, half]))
    return outs


def _attn_prompt_kernel(q_ref, kvc_ref, kvp_ref, bias_ref, o_ref, *, n_chunks):
    for cc, rows in enumerate(_attn_block(q_ref, kvc_ref, kvp_ref, bias_ref, n_chunks)):
        o_ref[pl.ds(cc * CHUNK, CHUNK), :] = rows.astype(o_ref.dtype)


def attn_prompt(pa, pb, sinks, *, qb, out_dtype=BF16):
    b, s, _ = pa.shape
    span = (WINDOW_CHUNKS + 1) * CHUNK
    qi = np.arange(CHUNK)
    kj = np.arange(span)
    dist = np.abs(qi[:, None] + WINDOW_CHUNKS * CHUNK - kj[None, :]).astype(np.float64)
    bias = jnp.stack([
        _attn_bias(dist, np.broadcast_to(c - WINDOW_CHUNKS + kj[None, :] // CHUNK >= 0, dist.shape), sinks)
        for c in range(WINDOW_CHUNKS + 1)])
    kv_w = 2 * ATTN_KV_DIM
    return pl.pallas_call(
        functools.partial(_attn_prompt_kernel, n_chunks=qb // CHUNK),
        grid=(b, s // qb),
        in_specs=[
            pl.BlockSpec((None, qb, ATTN_Q_DIM), lambda b, i: (b, i, PA_Q // ATTN_Q_DIM)),
            pl.BlockSpec((None, qb, kv_w), lambda b, i: (b, i, PB_K // kv_w)),
            pl.BlockSpec((None, WINDOW, kv_w), lambda b, i: (b, jnp.maximum(i * (qb // WINDOW) - 1, 0), PB_K // kv_w)),
            pl.BlockSpec(bias.shape, lambda b, i: (0,) * bias.ndim),
        ],
        out_specs=pl.BlockSpec((None, qb, ATTN_Q_DIM), lambda b, i: (b, i, 0)),
        out_shape=jax.ShapeDtypeStruct((b, s, ATTN_Q_DIM), out_dtype),
        compiler_params=_params("arbitrary", "arbitrary"),
        name="attn_prompt",
    )(pa, pb, pb, bias)


def _attn_sample_kernel(q_ref, kvn_ref, kc_ref, vc_ref, bias_ref, o_ref):
    kvn = kvn_ref[...]
    k_var = _kv_variants(jnp.concatenate([kc_ref[...], kvn[:, :ATTN_KV_DIM]], axis=0))
    v_var = _kv_variants(jnp.concatenate([vc_ref[...], kvn[:, ATTN_KV_DIM:]], axis=0))
    o_ref[...] = _attn_rows(q_ref[...], k_var, v_var, lambda h, half: bias_ref[h, half]).astype(o_ref.dtype)


def attn_sample(pa, pb, k_cache, v_cache, sinks, *, out_dtype=BF16):
    b, length, _ = pa.shape
    rows = k_cache.shape[1]
    qpos = PAST_LEN + np.arange(length)
    kpos = PAST_LEN - rows + np.arange(rows + length)
    qch, kch = qpos // CHUNK, kpos // CHUNK
    valid = (kch[None, :] <= qch[:, None]) & (kch[None, :] >= qch[:, None] - WINDOW_CHUNKS)
    dist = np.abs(qpos[:, None] - kpos[None, :]).astype(np.float64)
    bias = _attn_bias(dist, valid, sinks)
    kv_w = 2 * ATTN_KV_DIM
    cache = pl.BlockSpec((None, rows, ATTN_KV_DIM), lambda b: (b, 0, 0))
    return pl.pallas_call(
        _attn_sample_kernel,
        grid=(b,),
        in_specs=[
            pl.BlockSpec((None, length, ATTN_Q_DIM), lambda b: (b, 0, PA_Q // ATTN_Q_DIM)),
            pl.BlockSpec((None, length, kv_w), lambda b: (b, 0, PB_K // kv_w)),
            cache, cache,
            pl.BlockSpec(bias.shape, lambda b: (0,) * bias.ndim),
        ],
        out_specs=pl.BlockSpec((None, length, ATTN_Q_DIM), lambda b: (b, 0, 0)),
        out_shape=jax.ShapeDtypeStruct((b, length, ATTN_Q_DIM), out_dtype),
        compiler_params=_params("arbitrary"),
        name="attn_sample",
    )(pa, pb, k_cache, v_cache, bias)


def _causal_conv(tail_ref, prev_ref, cur_b, shift_ref, w_ref, b_ref, first):
    k = w_ref.shape[0]
    q = cur_b.shape[0]

    @pl.when(first)
    def _():
        tail_ref[pl.ds(0, SUBLANES), :] = prev_ref[...]

    cur = cur_b.astype(F32)
    tail_ref[pl.ds(SUBLANES, SUBLANES), :] = cur[:SUBLANES]
    acc = b_ref[...] + w_ref[k - 1:k, :] * cur
    for t in range(k - 1):
        shift = k - 1 - t
        moved = jnp.dot(shift_ref[shift - 1], cur_b, preferred_element_type=F32)
        head = tail_ref[pl.ds(SUBLANES - shift, SUBLANES), :]
        acc = acc + w_ref[t:t + 1, :] * jnp.concatenate([head, moved[SUBLANES:]], axis=0)
    tail_ref[pl.ds(0, SUBLANES), :] = cur[q - SUBLANES:]
    return acc


def _split3(x):
    hi = x.astype(BF16)
    r = x - hi.astype(F32)
    mid = r.astype(BF16)
    lo = (r - mid.astype(F32)).astype(BF16)
    return hi, mid, lo


def _ssd_block(x_ref, bc_ref, z_ref, dt_ref, xprev_ref, bcprev_ref, h0_ref, shift_ref,
               wx_ref, bx_ref, wbc_ref, bbc_ref, dtb_ref, alog_ref, dskip_ref, ng_ref,
               hout_ref, xpad, bcpad, h_scr, *, q, n_blocks, valid_len):
    i = pl.program_id(1)
    first = i == 0

    @pl.when(first)
    def _():
        h_scr[...] = h0_ref[...]

    xs = _silu(_causal_conv(xpad, xprev_ref, x_ref[...], shift_ref, wx_ref, bx_ref, first))
    bc = _silu(_causal_conv(bcpad, bcprev_ref, bc_ref[...], shift_ref, wbc_ref, bbc_ref, first))
    xs_b = xs.astype(BF16)

    dt = _softplus(dt_ref[...] + dtb_ref[...])
    if valid_len < q:
        row = lax.broadcasted_iota(jnp.int32, dt.shape, 0)
        dt = jnp.where(row < valid_len, dt, 0.0)
    da = dt * (-jnp.exp(alog_ref[...]))
    r_io = lax.broadcasted_iota(jnp.int32, (q, q), 0)
    c_io = lax.broadcasted_iota(jnp.int32, (q, q), 1)
    causal = c_io <= r_io
    tri = jnp.where(causal, 1.0, 0.0).astype(BF16)
    cum = sum(jnp.dot(tri, part, preferred_element_type=F32) for part in _split3(da))
    cum_t = cum.T
    dt_t = dt.T
    lane = lax.broadcasted_iota(jnp.int32, (1, LANES), 1)
    low_half = lane < SSD_HEAD_DIM

    y_pairs = []
    for g in range(SSD_GROUPS):
        b_g = bc[:, g * SSD_STATE:(g + 1) * SSD_STATE]
        c_g = bc[:, (SSD_GROUPS + g) * SSD_STATE:(SSD_GROUPS + g + 1) * SSD_STATE]
        cb = lax.dot_general(c_g.astype(BF16), b_g.astype(BF16), (((1,), (1,)), ((), ())),
                             preferred_element_type=F32)
        b_t = b_g.T
        for pr in range(g * SSD_HPG // 2, (g + 1) * SSD_HPG // 2):
            x_pair = xs_b[:, pr * LANES:(pr + 1) * LANES]
            h_pair = h_scr[pr]
            h_pair_b = h_pair.astype(BF16)
            ys, states, decays = [], [], []
            for j in (2 * pr, 2 * pr + 1):
                col = jnp.broadcast_to(cum[:, j:j + 1], (q, q))
                row = cum_t[j:j + 1, :]
                decay_in = jnp.exp(jnp.where(causal, col - row, -jnp.inf))
                m = (cb * decay_in * dt_t[j:j + 1, :]).astype(BF16)
                c_scaled = (c_g * jnp.exp(col[:, :SSD_STATE])).astype(BF16)
                ys.append(jnp.dot(m, x_pair, preferred_element_type=F32)
                          + jnp.dot(c_scaled, h_pair_b, preferred_element_type=F32))
                last = row[:, q - 1:q]
                w_row = dt_t[j:j + 1, :] * jnp.exp(last - row)
                states.append(jnp.dot((b_t * w_row).astype(BF16), x_pair, preferred_element_type=F32))
                decays.append(jnp.exp(last))
            h_scr[pr] = (jnp.where(low_half, decays[0], decays[1]) * h_pair
                         + jnp.where(low_half, states[0], states[1]))
            y_pairs.append(jnp.where(low_half, ys[0], ys[1]))
    y = jnp.concatenate(y_pairs, axis=1) + dskip_ref[...] * xs
    y = y * _silu(z_ref[...].astype(F32))
    half = SSD_INNER // SSD_GROUPS
    normed = []
    for g in range(SSD_GROUPS):
        yg = y[:, g * half:(g + 1) * half]
        normed.append(yg * lax.rsqrt(jnp.mean(yg * yg, axis=-1, keepdims=True) + EPS))
    @pl.when(i == n_blocks - 1)
    def _():
        hout_ref[...] = h_scr[...]

    return jnp.concatenate(normed, axis=1) * ng_ref[...]


def _ssd_kernel(*refs, **static):
    y_ref = refs[16]
    y_ref[...] = _ssd_block(*refs[:16], *refs[17:], **static).astype(y_ref.dtype)


def ssd_mixer(pa, pb, x_prev, bc_prev, h0, conv_w, conv_b, dt_bias, a_log, d_skip, norm_g, *, q, valid_len=None,
              out_dtype=BF16):
    b, s, _ = pa.shape
    valid_len = q if valid_len is None else valid_len
    parts = _ssd_parts(pa, pb, x_prev, bc_prev, h0, conv_w, conv_b, dt_bias, a_log, d_skip, norm_g, q)
    return pl.pallas_call(
        functools.partial(_ssd_kernel, q=q, n_blocks=s // q, valid_len=valid_len),
        grid=(b, s // q),
        in_specs=parts.in_specs,
        out_specs=[pl.BlockSpec((None, q, SSD_INNER), lambda b, i: (b, i, 0)), parts.state_spec],
        out_shape=[jax.ShapeDtypeStruct((b, s, SSD_INNER), out_dtype), parts.state_shape],
        scratch_shapes=parts.scratch,
        compiler_params=_params("arbitrary", "arbitrary"),
        name="ssd_mixer",
    )(*parts.args)


class _SsdParts(NamedTuple):
    in_specs: list
    args: list
    scratch: list
    state_spec: Any
    state_shape: Any


def _ssd_parts(pa, pb, x_prev, bc_prev, h0, conv_w, conv_b, dt_bias, a_log, d_skip, norm_g, q):
    b = pa.shape[0]
    pad128 = lambda v: jnp.pad(v.astype(F32), (0, LANES - v.shape[0])).reshape(1, LANES)
    col = lambda width, off: pl.BlockSpec((None, q, width), lambda b, i: (b, i, off // width))
    const = lambda shape: pl.BlockSpec(shape, lambda b, i: (0,) * len(shape))
    per_b = lambda shape: pl.BlockSpec((None,) + shape, lambda b, i: (b,) + (0,) * len(shape))
    h_shape = (SSD_HEADS // 2, SSD_STATE, LANES)
    shifts = np.stack([np.eye(q, k=-s, dtype=np.float32) for s in range(1, SSD_CONV)])
    in_specs = [
        col(SSD_INNER, PA_X), col(SSD_BC_DIM, PA_BC), col(SSD_INNER, PA_Z), col(LANES, PB_DT),
        per_b((SUBLANES, SSD_INNER)), per_b((SUBLANES, SSD_BC_DIM)), per_b(h_shape),
        const((SSD_CONV - 1, q, q)), const((SSD_CONV, SSD_INNER)), const((1, SSD_INNER)),
        const((SSD_CONV, SSD_BC_DIM)), const((1, SSD_BC_DIM)),
        const((1, LANES)), const((1, LANES)), const((1, SSD_INNER)), const((1, SSD_INNER)),
    ]
    args = [pa, pa, pa, pb, x_prev, bc_prev, h0, jnp.asarray(shifts, BF16),
            conv_w[:, :SSD_INNER], conv_b[:SSD_INNER].reshape(1, -1),
            conv_w[:, SSD_INNER:], conv_b[SSD_INNER:].reshape(1, -1),
            pad128(dt_bias), pad128(a_log),
            jnp.repeat(d_skip.astype(F32), SSD_HEAD_DIM).reshape(1, -1), norm_g.astype(F32).reshape(1, -1)]
    scratch = [pltpu.VMEM((2 * SUBLANES, SSD_INNER), F32), pltpu.VMEM((2 * SUBLANES, SSD_BC_DIM), F32),
               pltpu.VMEM(h_shape, F32)]
    return _SsdParts(in_specs, args, scratch, per_b(h_shape), jax.ShapeDtypeStruct((b,) + h_shape, F32))


def _pair_state(h):
    b = h.shape[0]
    h = h.astype(F32).reshape(b, SSD_HEADS // 2, 2, SSD_HEAD_DIM, SSD_STATE)
    return h.transpose(0, 1, 4, 2, 3).reshape(b, SSD_HEADS // 2, SSD_STATE, 2 * SSD_HEAD_DIM)


def _unpair_state(h):
    b = h.shape[0]
    h = h.reshape(b, SSD_HEADS // 2, SSD_STATE, 2, SSD_HEAD_DIM)
    return h.transpose(0, 1, 3, 4, 2).reshape(b, SSD_HEADS, SSD_HEAD_DIM, SSD_STATE)


def _tail_rows(prev):
    return jnp.pad(prev.astype(F32), ((0, 0), (SUBLANES - prev.shape[1], 0), (0, 0)))


def _outproj_kernel(x_ref, a_ref, y_ref, gt_ref, wa_ref, wy_ref, o_ref):
    m = (jnp.dot(a_ref[...].astype(BF16), wa_ref[...], preferred_element_type=F32)
         + jnp.dot(y_ref[...].astype(BF16), wy_ref[...], preferred_element_type=F32))
    o_ref[...] = x_ref[...] + gt_ref[...] * m


def mix0_out(x, attn, y, gt, w_out, *, tm):
    b, s, d = x.shape
    da, dy = attn.shape[2], y.shape[2]
    mod_rows = 1 if gt.shape[1] == 1 else tm
    x_spec = pl.BlockSpec((None, tm, d), lambda b, i: (b, i, 0))
    return pl.pallas_call(
        _outproj_kernel,
        grid=(b, s // tm),
        in_specs=[
            x_spec,
            pl.BlockSpec((None, tm, da), lambda b, i: (b, i, 0)),
            pl.BlockSpec((None, tm, dy), lambda b, i: (b, i, 0)),
            _mod_spec(mod_rows, d),
            pl.BlockSpec((da, d), lambda b, i: (0, 0)),
            pl.BlockSpec((dy, d), lambda b, i: (1, 0)),
        ],
        out_specs=x_spec,
        out_shape=jax.ShapeDtypeStruct(x.shape, F32),
        compiler_params=_params("arbitrary", "arbitrary"),
        name="mix0_out",
    )(x, attn, y, gt, w_out, w_out)


SCONV_TC = 512


def _sconv_kernel(x_ref, xn_ref, g_ref, sh_ref, sc_ref, gt_ref, prev_ref, cw_ref, win_ref, wout_ref,
                  o_ref, buf_ref, h_even, h_odd, rs_scr, vpad, carry, *, tm):
    b, i = pl.program_id(0), pl.program_id(1)
    n_b, n_t = pl.num_programs(0), pl.num_programs(1)
    tile = b * n_t + i
    bn, _ = _next_tile(b, i, n_b, n_t)
    d = x_ref.shape[1]
    k = cw_ref.shape[0]
    tc = SCONV_TC
    n_c = d // tc

    @pl.when(i == 0)
    def _():
        carry[...] = prev_ref[...]

    @pl.when(tile == 0)
    def _():
        _modnorm_store(h_even, x_ref, g_ref, sc_ref.at[b], sh_ref.at[b], rs_scr)

    def step(h_cur, h_nxt):
        tie = _modnorm_rows(h_nxt, 0, xn_ref, g_ref, sc_ref.at[bn], sh_ref.at[bn])
        h = h_cur[...]

        def in_proj(c):
            return [jnp.dot(h, win_ref[:, part * d + c * tc:part * d + (c + 1) * tc], preferred_element_type=F32)
                    for part in (1, 2, 0)]

        def mix(c, gate_c, xi, gate_b):
            cols = slice(c * tc, (c + 1) * tc)
            v = gate_c * xi
            vpad[pl.ds(0, SUBLANES), cols] = carry[:, cols]
            vpad[pl.ds(SUBLANES, tm), cols] = v
            u = cw_ref[k - 1:k, cols] * v
            for t in range(k - 1):
                u = u + cw_ref[t:t + 1, cols] * vpad[pl.ds(SUBLANES - (k - 1) + t, tm), cols]
            carry[:, cols] = vpad[pl.ds(tm, SUBLANES), cols]
            return gate_b * u

        acc = None
        pending = in_proj(0)
        for c in range(n_c):
            following = in_proj(c + 1) if c + 1 < n_c else None
            r = mix(c, *pending)
            if c == 0:
                r = _tied(r, tie)
            part = jnp.dot(r.astype(BF16), wout_ref[c * tc:(c + 1) * tc, :], preferred_element_type=F32)
            acc = part if acc is None else acc + part
            pending = following
        o_ref[...] = x_ref[...] + gt_ref[b] * acc
        buf_ref[...] = carry[...]

    pl.when(tile % 2 == 0)(functools.partial(step, h_even, h_odd))
    pl.when(tile % 2 == 1)(functools.partial(step, h_odd, h_even))


def sconv_mixer(x, g, sh, sc, gt, prev, conv_w, w_in, w_out, *, tm):
    b, s, d = x.shape
    n_t = s // tm
    x_spec = pl.BlockSpec((None, tm, d), lambda b, i: (b, i, 0))
    next_spec = pl.BlockSpec((None, tm, d), lambda bb, i: _next_tile(bb, i, b, n_t) + (0,))
    buf_spec = pl.BlockSpec((None, SUBLANES, d), lambda b, i: (b, 0, 0))
    const = lambda shape: pl.BlockSpec(shape, lambda b, i: (0,) * len(shape))
    resident = lambda shape: pl.BlockSpec(shape, lambda b, i: (0, 0), pipeline_mode=pl.Buffered(1))
    return pl.pallas_call(
        functools.partial(_sconv_kernel, tm=tm),
        grid=(b, n_t),
        in_specs=[
            x_spec, next_spec, const((1, d)), const(sh.shape), const(sc.shape), const(gt.shape), buf_spec,
            const((SCONV_WIDTH, d)), resident((d, 3 * d)), resident((d, d)),
        ],
        out_specs=[x_spec, buf_spec],
        out_shape=[jax.ShapeDtypeStruct(x.shape, F32), jax.ShapeDtypeStruct((b, SUBLANES, d), F32)],
        scratch_shapes=[
            pltpu.VMEM((tm, d), BF16), pltpu.VMEM((tm, d), BF16), pltpu.VMEM((tm, LANES), F32),
            pltpu.VMEM((tm + SUBLANES, d), F32), pltpu.VMEM((SUBLANES, d), F32),
        ],
        compiler_params=_params("arbitrary", "arbitrary"),
        name="sconv_mixer",
    )(x, x, g.reshape(1, d), sh, sc, gt, prev, conv_w.astype(F32), w_in, w_out)


def _arrange_w_in0(w):
    i1 = ATTN_Q_DIM
    i2 = i1 + ATTN_KV_DIM
    i3 = i2 + ATTN_KV_DIM
    i4 = i3 + SSD_INNER
    i5 = i4 + SSD_CONV_DIM
    q, k, v, z, xbc, dt = (w[:, a:b] for a, b in ((0, i1), (i1, i2), (i2, i3), (i3, i4), (i4, i5), (i5, w.shape[1])))
    pad = jnp.zeros((w.shape[0], PB_WIDTH - PB_DT - dt.shape[1]), w.dtype)
    return jnp.concatenate([q, z, xbc, k, v, dt, pad], axis=1).astype(BF16)


def _split_mod(mod):
    return [m[:, None, :] for m in jnp.split(mod, N_MOD, axis=-1)]


def _per_token(m, length):
    b, _, d = m.shape
    return jnp.broadcast_to(m, (b, length, d)).reshape(1, b * length, d)


class _Tiles(NamedTuple):
    ffn: int
    proj: int
    mix_out: int
    sconv: int
    ssd: int
    attn: int


def _tiles(batch, seq, flat):
    if flat:
        rows = batch * seq
        return _Tiles(ffn=rows, proj=rows, mix_out=rows, sconv=seq, ssd=LANES, attn=seq)
    return _Tiles(ffn=min(1024, seq), proj=min(512, seq), mix_out=min(1024, seq), sconv=min(256, seq),
                  ssd=min(256, seq), attn=min(1024, seq))


def _trunk(x, mods, weights, cache):
    b, s, d = x.shape
    t = _tiles(b, s, cache is not None)
    tm, tm_proj, tm_mix, q_ssd, qb_attn = t.ffn, t.proj, t.mix_out, t.ssd, t.attn
    flat = cache is not None
    if flat:
        as_rows = lambda t: t.reshape(1, b * s, t.shape[-1])
        mod_of = lambda m: _per_token(m, s)
    else:
        as_rows = lambda t: t
        mod_of = lambda m: m
    unrow = lambda t: t.reshape(b, s, t.shape[-1])
    w = weights

    def ffn(xr, l, k, sh, sc, gt, final_g=None):
        return ffn_half(xr, w["norm_g"][l, 2 * k], mod_of(sh), mod_of(sc), mod_of(gt),
                        w["ffn_gate"], w["ffn_up"], w["ffn_down"], l, k, final_g, tm=tm)

    sh1, sc1, g1, sh2, sc2, g2, sh3, sc3, g3 = mods[0]
    xr = ffn(as_rows(x), 0, 0, sh1, sc1, g1)
    pa, pb = modnorm_proj(xr, w["norm_g"][0, 1], mod_of(sh2), mod_of(sc2), w["w_in0"], tm=tm_proj,
                          tail_width=PB_WIDTH)
    pa, pb = unrow(pa), unrow(pb)
    if cache is None:
        x_prev = jnp.zeros((b, SUBLANES, SSD_INNER), F32)
        bc_prev = jnp.zeros((b, SUBLANES, SSD_BC_DIM), F32)
        h0 = jnp.zeros((b, SSD_HEADS // 2, SSD_STATE, LANES), F32)
        attn = attn_prompt(pa, pb, w["sinks"], qb=qb_attn)
        y, h_new = ssd_mixer(pa, pb, x_prev, bc_prev, h0, *w["ssd"], q=q_ssd)
        xr = mix0_out(xr, attn, y, g2, w["w_out0"], tm=tm_mix)
        new_kv = pb[:, s - WINDOW:]
        sconv_prev = jnp.zeros((b, SUBLANES, d), F32)
    else:
        k_cache, v_cache, h_state, conv_prev, sconv_state = cache
        rows = k_cache.shape[1]
        attn = attn_sample(pa, pb, k_cache.reshape(b, rows, ATTN_KV_DIM), v_cache.reshape(b, rows, ATTN_KV_DIM),
                           w["sinks"])
        tail = _tail_rows(conv_prev)
        pad_rows = lambda t: jnp.pad(t, ((0, 0), (0, q_ssd - s), (0, 0)))
        y, h_new = ssd_mixer(pad_rows(pa), pad_rows(pb), tail[:, :, :SSD_INNER], tail[:, :, SSD_INNER:],
                             _pair_state(h_state), *w["ssd"], q=q_ssd, valid_len=s)
        xr = mix0_out(xr, as_rows(attn), as_rows(y[:, :s]), mod_of(g2), w["w_out0"], tm=tm_mix)
        new_kv = pb
        sconv_prev = _tail_rows(sconv_state)
    xbc_raw = pa[:, s - (SSD_CONV - 1):, PA_X:PA_X + SSD_CONV_DIM].astype(F32)
    kv_shape = (b, new_kv.shape[1], ATTN_KV_HEADS, HEAD_DIM)
    states0 = (new_kv[:, :, PB_K:PB_K + ATTN_KV_DIM].reshape(kv_shape),
               new_kv[:, :, PB_V:PB_V + ATTN_KV_DIM].reshape(kv_shape), _unpair_state(h_new), xbc_raw)
    xr = ffn(xr, 0, 1, sh3, sc3, g3)

    sh1, sc1, g1, sh2, sc2, g2, sh3, sc3, g3 = mods[1]
    xr = ffn(xr, 1, 0, sh1, sc1, g1)
    xm, buf = sconv_mixer(unrow(xr), w["norm_g"][1, 1], sh2, sc2, g2, sconv_prev, w["sconv_w"], w["w_in1"],
                          w["w_out1"], tm=t.sconv)
    sconv_new = buf[:, SUBLANES - (SCONV_WIDTH - 1):]
    y_out = unrow(ffn(as_rows(xm), 1, 1, sh3, sc3, g3, final_g=w["final_g"]))
    return y_out, states0, sconv_new


def kernel(x_prompt, x_sample, c_prompt, c_sample, cache_swa_k, cache_swa_v, state_ssd, state_ssd_conv, state_sconv, norm_g, w_ada, b_ada, w_ffn_gate, w_ffn_up, w_ffn_down, w_in_mix0, w_out_mix0, attn_sinks, ssd_conv_w, ssd_conv_b, ssd_dt_bias, ssd_a_log, ssd_d, ssd_norm_g, w_in_mix1, sconv_w, w_out_mix1, final_norm_g):
    bp, bs = c_prompt.shape[0], c_sample.shape[0]
    c_all = jnp.concatenate([c_prompt, c_sample], axis=0)
    c_rows = -(-c_all.shape[0] // SUBLANES) * SUBLANES
    c_all = jnp.pad(c_all, ((0, c_rows - c_all.shape[0]), (0, 0)))
    mod = ada_modulation(c_all, w_ada, b_ada)
    mods_p = [_split_mod(mod[l, :bp]) for l in range(mod.shape[0])]
    mods_s = [_split_mod(mod[l, bp:bp + bs]) for l in range(mod.shape[0])]

    weights = {
        "norm_g": norm_g,
        "ffn_gate": w_ffn_gate.astype(BF16), "ffn_up": w_ffn_up.astype(BF16), "ffn_down": w_ffn_down.astype(BF16),
        "w_in0": _arrange_w_in0(w_in_mix0[0]), "w_out0": w_out_mix0[0].astype(BF16),
        "sinks": attn_sinks[0],
        "ssd": (ssd_conv_w[0].astype(F32), ssd_conv_b[0].astype(F32), ssd_dt_bias[0], ssd_a_log[0], ssd_d[0],
                ssd_norm_g[0]),
        "w_in1": w_in_mix1[0].astype(BF16), "sconv_w": sconv_w[0], "w_out1": w_out_mix1[0].astype(BF16),
        "final_g": final_norm_g,
    }
    y_p, (k_p, v_p, h_p, cv_p), sc_p = _trunk(x_prompt, mods_p, weights, None)
    cache = (cache_swa_k[0], cache_swa_v[0], state_ssd[0], state_ssd_conv[0], state_sconv[0])
    y_s, (k_s, v_s, h_s, cv_s), sc_s = _trunk(x_sample, mods_s, weights, cache)
    stack = lambda t: t[None]
    return (y_p, y_s, stack(k_p), stack(v_p), stack(h_p), stack(cv_p), stack(sc_p),
            stack(k_s), stack(v_s), stack(h_s), stack(cv_s), stack(sc_s))
```

```python
import functools
from typing import Any, NamedTuple

import numpy as np
import jax
import jax.numpy as jnp
from jax import lax
from jax.experimental import pallas as pl
from jax.experimental.pallas import tpu as pltpu

F32 = jnp.float32
BF16 = jnp.bfloat16

EPS = 1e-6
CHUNK = 64
PAST_LEN = 2048
ATTN_HEADS = 16
ATTN_KV_HEADS = 2
ATTN_GROUP = ATTN_HEADS // ATTN_KV_HEADS
HEAD_DIM = 64
WINDOW = 128
WINDOW_CHUNKS = WINDOW // CHUNK
ATTN_Q_DIM = ATTN_HEADS * HEAD_DIM
ATTN_KV_DIM = ATTN_KV_HEADS * HEAD_DIM
SSD_HEADS = 16
SSD_HEAD_DIM = 64
SSD_INNER = SSD_HEADS * SSD_HEAD_DIM
SSD_STATE = 128
SSD_GROUPS = 2
SSD_HPG = SSD_HEADS // SSD_GROUPS
SSD_CONV = 4
SSD_BC_DIM = 2 * SSD_GROUPS * SSD_STATE
SSD_CONV_DIM = SSD_INNER + SSD_BC_DIM
SCONV_WIDTH = 3
N_MOD = 9

LANES = 128
SUBLANES = 8
MXU_DIM = 256
VMEM_LIMIT = 62 * 1024 * 1024

PROJ_TN = 512
PA_Q = 0
PA_Z = PA_Q + ATTN_Q_DIM
PA_X = PA_Z + SSD_INNER
PA_BC = PA_X + SSD_INNER
PA_WIDTH = PA_BC + SSD_BC_DIM
PB_K = 0
PB_V = PB_K + ATTN_KV_DIM
PB_DT = PB_V + ATTN_KV_DIM
PB_WIDTH = PROJ_TN
ATTN_KEY_PAD = MXU_DIM
assert HEAD_DIM * 2 == LANES and ATTN_KV_DIM == LANES and PA_WIDTH % PROJ_TN == 0


def _params(*sem):
    return pltpu.CompilerParams(dimension_semantics=sem, vmem_limit_bytes=VMEM_LIMIT)


def _silu(x):
    return x / (1.0 + jnp.exp(-x))


def _softplus(x):
    return jnp.maximum(x, 0.0) + jnp.log(1.0 + jnp.exp(-jnp.abs(x)))


NORM_ROWS = 2 * SUBLANES


def _modnorm_store(h_ref, x_ref, g_ref, sc_ref, sh_ref, rs_scr):
    tm, d = x_ref.shape
    for c in range(tm // SUBLANES):
        r = pl.ds(c * SUBLANES, SUBLANES)
        x = x_ref[r, :]
        ms = jnp.mean(x * x, axis=-1, keepdims=True)
        rs_scr[r, :] = jnp.broadcast_to(lax.rsqrt(ms + EPS), (SUBLANES, LANES))
    per_row = sc_ref.shape[0] != 1
    scale = None if per_row else g_ref[...] * (1.0 + sc_ref[...])

    def body(c, carry):
        r = pl.ds(pl.multiple_of(c * NORM_ROWS, NORM_ROWS), NORM_ROWS)
        rs = jnp.concatenate([rs_scr[r, :]] * (d // LANES), axis=1)
        y = x_ref[r, :] * rs
        if per_row:
            h = y * (g_ref[...] * (1.0 + sc_ref[r, :])) + sh_ref[r, :]
        else:
            h = y * scale + sh_ref[...]
        h_ref[r, :] = h.astype(BF16)
        return carry

    lax.fori_loop(0, tm // NORM_ROWS, body, 0, unroll=2 if tm >= 2 * NORM_ROWS else 1)


def _mod_spec(mod_rows, d):
    if mod_rows == 1:
        return pl.BlockSpec((None, 1, d), lambda b, i, *_: (b, 0, 0))
    return pl.BlockSpec((None, mod_rows, d), lambda b, i, *_: (b, i, 0))


def _ada_kernel(c_ref, w_ref, b_ref, o_ref):
    c = c_ref[...]
    s = _silu(c).astype(BF16)
    o_ref[...] = jnp.dot(s, w_ref[...].astype(BF16), preferred_element_type=F32) + b_ref[...]


def ada_modulation(c, w_ada, b_ada, tn=1024):
    n_layers, d, n = w_ada.shape
    r = c.shape[0]
    return pl.pallas_call(
        _ada_kernel,
        grid=(n_layers, n // tn),
        in_specs=[
            pl.BlockSpec((r, d), lambda l, j: (0, 0)),
            pl.BlockSpec((None, d, tn), lambda l, j: (l, 0, j)),
            pl.BlockSpec((None, 1, tn), lambda l, j: (l, 0, j)),
        ],
        out_specs=pl.BlockSpec((None, r, tn), lambda l, j: (l, 0, j)),
        out_shape=jax.ShapeDtypeStruct((n_layers, r, n), F32),
        compiler_params=_params("arbitrary", "arbitrary"),
        name="ada_modulation",
    )(c, w_ada, b_ada.reshape(n_layers, 1, n))


def _rmsnorm_inplace(o_ref, rs_scr, fg_ref):
    tm, d = o_ref.shape
    n_steps = tm // NORM_ROWS
    for c in range(tm // SUBLANES):
        r = pl.ds(c * SUBLANES, SUBLANES)
        xn = o_ref[r, :]
        ms = jnp.mean(xn * xn, axis=-1, keepdims=True)
        rs_scr[r, :] = jnp.broadcast_to(lax.rsqrt(ms + EPS), (SUBLANES, LANES))

    def scale(c, carry):
        r = pl.ds(pl.multiple_of(c * NORM_ROWS, NORM_ROWS), NORM_ROWS)
        rs = jnp.concatenate([rs_scr[r, :]] * (d // LANES), axis=1)
        o_ref[r, :] = o_ref[r, :] * rs * fg_ref[...]
        return carry

    lax.fori_loop(0, n_steps, scale, 0, unroll=2 if n_steps >= 2 else 1)


NORM_SPLIT = 8


def _modnorm_rows(h_ref, row0, x_ref, g_ref, sc_ref, sh_ref):
    rows = x_ref.shape[0]
    per_row = sc_ref.shape[0] != 1
    if not per_row:
        scale = g_ref[...] * (1.0 + sc_ref[...])
    tie = None
    for c in range(rows // NORM_ROWS):
        r = pl.ds(c * NORM_ROWS, NORM_ROWS)
        x = x_ref[r, :]
        y = x * lax.rsqrt(jnp.mean(x * x, axis=-1, keepdims=True) + EPS)
        if per_row:
            hv = y * (g_ref[...] * (1.0 + sc_ref[r, :])) + sh_ref[r, :]
        else:
            hv = y * scale + sh_ref[...]
        h_ref[pl.ds(row0 + c * NORM_ROWS, NORM_ROWS), :] = hv.astype(BF16)
        zero = _zero_tie(hv)
        tie = zero if tie is None else tie | zero
    return tie


def _zero_tie(v):
    bits = pltpu.bitcast(v[:NORM_ROWS, :LANES], jnp.uint32)
    return (bits >> 16) >> 16


def _tied(v, tie):
    r, c = tie.shape
    head = v[:r, :c] + tie.astype(jnp.int32).astype(F32).astype(v.dtype)
    top = jnp.concatenate([head, v[:r, c:]], axis=1)
    return top if v.shape[0] == r else jnp.concatenate([top, v[r:]], axis=0)


def _next_tile(b, i, n_b, n_t):
    nxt = jnp.minimum(b * n_t + i + 1, n_b * n_t - 1)
    return nxt // n_t, nxt % n_t


def _ffn_kernel(x_ref, xn_ref, g_ref, sh_ref, sc_ref, gt_ref, wg_ref, wu_ref, wd_ref, *rest, n_ff, final):
    if final:
        fg_ref, o_ref, h_even, h_odd, rs_scr = rest
    else:
        fg_ref = None
        o_ref, h_even, h_odd, rs_scr = rest
    b, i, j = pl.program_id(0), pl.program_id(1), pl.program_id(2)
    n_b, n_t = pl.num_programs(0), pl.num_programs(1)
    tile = b * n_t + i
    tm, rows_n = x_ref.shape[0], xn_ref.shape[0]
    per_row = sc_ref.shape[1] != 1

    def mod_rows(ref, bb, row0, rows):
        return ref.at[bb, pl.ds(row0, rows)] if per_row else ref.at[bb]

    cur = functools.partial(mod_rows, bb=b, row0=pl.multiple_of(i * tm, tm), rows=tm)

    @pl.when((tile == 0) & (j == 0))
    def _():
        _modnorm_store(h_even, x_ref, g_ref, cur(sc_ref), cur(sh_ref), rs_scr)

    bn, tn = _next_tile(b, i, n_b, n_t)
    part = jnp.minimum(j, NORM_SPLIT - 1)
    nxt = functools.partial(mod_rows, bb=bn, row0=pl.multiple_of(tn * tm + part * rows_n, rows_n), rows=rows_n)

    def step(h_cur, h_nxt, last):
        tie = _modnorm_rows(h_nxt, pl.multiple_of(part * rows_n, rows_n), xn_ref, g_ref, nxt(sc_ref), nxt(sh_ref))
        h = h_cur[...]
        half = wg_ref.shape[1] // 2
        acts = []
        for c in range(2):
            cols = slice(c * half, (c + 1) * half)
            g = jnp.dot(h, wg_ref[:, cols], preferred_element_type=F32)
            u = jnp.dot(h, wu_ref[:, cols], preferred_element_type=F32)
            act = _silu(g) * u
            if c == 0:
                act = _tied(act, tie)
            acts.append(act.astype(BF16))
        total = jnp.where(j == 0, 0.0, o_ref[...])
        for c in range(2):
            total = total + jnp.dot(acts[c], wd_ref[c * half:(c + 1) * half, :], preferred_element_type=F32)
        if last:
            total = x_ref[...] + 0.5 * cur(gt_ref)[...] * total
        o_ref[...] = total

    for parity, (h_cur, h_nxt) in enumerate(((h_even, h_odd), (h_odd, h_even))):
        pl.when((tile % 2 == parity) & (j < n_ff - 1))(functools.partial(step, h_cur, h_nxt, False))
        pl.when((tile % 2 == parity) & (j == n_ff - 1))(functools.partial(step, h_cur, h_nxt, True))

    if final:
        @pl.when(j == n_ff - 1)
        def _():
            _rmsnorm_inplace(o_ref, rs_scr, fg_ref)


FFN_TF = 512


def ffn_half(x, g, sh, sc, gt, wg, wu, wd, layer, half, final_g=None, *, tm, tf=FFN_TF):
    b, s, d = x.shape
    n_ff = wg.shape[-1] // tf
    n_t = s // tm
    assert n_ff >= NORM_SPLIT and tm % (NORM_SPLIT * NORM_ROWS) == 0
    rows_n = tm // NORM_SPLIT
    final = final_g is not None
    x_spec = pl.BlockSpec((None, tm, d), lambda b, i, j: (b, i, 0))

    def next_rows(bb, i, j):
        bn, tn = _next_tile(bb, i, b, n_t)
        return bn, tn * NORM_SPLIT + jnp.minimum(j, NORM_SPLIT - 1), 0

    whole = pl.BlockSpec(sh.shape, lambda b, i, j: (0, 0, 0))
    in_specs = [
        x_spec,
        pl.BlockSpec((None, rows_n, d), next_rows),
        pl.BlockSpec((1, d), lambda b, i, j: (0, 0)),
        whole, whole, whole,
        pl.BlockSpec((None, None, d, tf), lambda b, i, j: (layer, half, 0, j)),
        pl.BlockSpec((None, None, d, tf), lambda b, i, j: (layer, half, 0, j)),
        pl.BlockSpec((None, None, tf, d), lambda b, i, j: (layer, half, j, 0)),
    ]
    args = [x, x, g.reshape(1, d), sh, sc, gt, wg, wu, wd]
    if final:
        in_specs.append(pl.BlockSpec((1, d), lambda b, i, j: (0, 0)))
        args.append(final_g.reshape(1, d))
    return pl.pallas_call(
        functools.partial(_ffn_kernel, n_ff=n_ff, final=final),
        grid=(b, n_t, n_ff),
        in_specs=in_specs,
        out_specs=x_spec,
        out_shape=jax.ShapeDtypeStruct(x.shape, F32),
        scratch_shapes=[pltpu.VMEM((tm, d), BF16), pltpu.VMEM((tm, d), BF16), pltpu.VMEM((tm, LANES), F32)],
        compiler_params=_params("arbitrary", "arbitrary", "arbitrary"),
        name="ffn_half_final" if final else "ffn_half",
    )(*args)


def _proj_kernel(x0_ref, xn_ref, g_ref, sh_ref, sc_ref, w_ref, main_ref, tail_ref, h_even, h_odd, rs_scr):
    b, i = pl.program_id(0), pl.program_id(1)
    n_b, n_t = pl.num_programs(0), pl.num_programs(1)
    tile = b * n_t + i
    tm = xn_ref.shape[0]
    per_row = sc_ref.shape[1] != 1

    def mod_rows(ref, bb, tt):
        return ref.at[bb, pl.ds(pl.multiple_of(tt * tm, tm), tm)] if per_row else ref.at[bb]

    @pl.when(tile == 0)
    def _():
        _modnorm_store(h_even, x0_ref, g_ref, mod_rows(sc_ref, 0, 0), mod_rows(sh_ref, 0, 0), rs_scr)

    bn, tn = _next_tile(b, i, n_b, n_t)

    def step(h_cur, h_nxt):
        tie = _modnorm_rows(h_nxt, 0, xn_ref, g_ref, mod_rows(sc_ref, bn, tn), mod_rows(sh_ref, bn, tn))
        h = h_cur[...]
        n_main = main_ref.shape[1]
        for c in range(n_main // PROJ_TN):
            cols = slice(c * PROJ_TN, (c + 1) * PROJ_TN)
            main_ref[:, cols] = jnp.dot(h, w_ref[:, cols], preferred_element_type=F32).astype(main_ref.dtype)
        tail_ref[...] = jnp.dot(_tied(h, tie), w_ref[:, n_main:], preferred_element_type=F32)

    pl.when(tile % 2 == 0)(functools.partial(step, h_even, h_odd))
    pl.when(tile % 2 == 1)(functools.partial(step, h_odd, h_even))


def modnorm_proj(x, g, sh, sc, w, *, tm, tail_width):
    b, s, d = x.shape
    n = w.shape[1]
    n_t = s // tm
    const = lambda shape: pl.BlockSpec(shape, lambda b, i: (0,) * len(shape))
    out_row = lambda width: pl.BlockSpec((None, tm, width), lambda b, i: (b, i, 0))
    return pl.pallas_call(
        _proj_kernel,
        grid=(b, n_t),
        in_specs=[
            pl.BlockSpec((None, tm, d), lambda b, i: (0, 0, 0)),
            pl.BlockSpec((None, tm, d), lambda bb, i: _next_tile(bb, i, b, n_t) + (0,)),
            const((1, d)), const(sh.shape), const(sc.shape),
            pl.BlockSpec((d, n), lambda b, i: (0, 0), pipeline_mode=pl.Buffered(1)),
        ],
        out_specs=[out_row(n - tail_width), out_row(tail_width)],
        out_shape=[jax.ShapeDtypeStruct((b, s, n - tail_width), BF16),
                   jax.ShapeDtypeStruct((b, s, tail_width), F32)],
        scratch_shapes=[pltpu.VMEM((tm, d), BF16), pltpu.VMEM((tm, d), BF16), pltpu.VMEM((tm, LANES), F32)],
        compiler_params=_params("arbitrary", "arbitrary"),
        name="modnorm_proj",
    )(x, x, g.reshape(1, d), sh, sc, w)


def _kv_variants(t):
    low = lax.broadcasted_iota(jnp.int32, (1, LANES), 1) < HEAD_DIM
    swapped = pltpu.roll(t, HEAD_DIM, 1)
    zero = jnp.zeros_like(t)
    return {
        (0, 0): jnp.where(low, t, zero).astype(BF16), (0, 1): jnp.where(low, zero, swapped).astype(BF16),
        (1, 0): jnp.where(low, swapped, zero).astype(BF16), (1, 1): jnp.where(low, zero, t).astype(BF16),
    }


def _pad_rows(x, rows):
    return jnp.concatenate([x, jnp.zeros((rows - x.shape[0], x.shape[1]), x.dtype)], axis=0)


def _attn_rows(q_rows, k_var, v_var, bias_of):
    r = q_rows.shape[0]
    blocks_per_kv = ATTN_GROUP // 2
    low = lax.broadcasted_iota(jnp.int32, (ATTN_KEY_PAD, LANES), 1) < HEAD_DIM
    ones = (jnp.where(low, 1.0, 0.0).astype(BF16), jnp.where(low, 0.0, 1.0).astype(BF16))
    outs = []
    for h in range(ATTN_KV_HEADS):
        stack = jnp.concatenate(
            [q_rows[:, (h * blocks_per_kv + t) * LANES:(h * blocks_per_kv + t + 1) * LANES]
             for t in range(blocks_per_kv)], axis=0)
        stack = (stack.astype(F32) * (HEAD_DIM ** -0.5)).astype(BF16)
        keys = jnp.concatenate([_pad_rows(k_var[h, half], ATTN_KEY_PAD) for half in range(2)], axis=0)
        s_all = lax.dot_general(stack, keys, (((1,), (1,)), ((), ())), preferred_element_type=F32)
        probs, rhs = [], []
        for half in range(2):
            s = s_all[:, half * ATTN_KEY_PAD:(half + 1) * ATTN_KEY_PAD] + bias_of(h, half)
            probs.append(jnp.exp(s - jnp.max(s, axis=1, keepdims=True)).astype(BF16))
            rhs.append(jnp.concatenate([_pad_rows(v_var[h, half], ATTN_KEY_PAD), ones[half]], axis=1))
        res = jnp.dot(jnp.concatenate(probs, axis=1), jnp.concatenate(rhs, axis=0), preferred_element_type=F32)
        blk = res[:, :LANES] / res[:, LANES:]
        outs.extend(blk[t * r:(t + 1) * r] for t in range(blocks_per_kv))
    return jnp.concatenate(outs, axis=1)


def _alibi_slopes():
    return 2.0 ** (-8.0 * np.arange(1, ATTN_HEADS + 1) / ATTN_HEADS)


def _attn_bias(dist, valid, sinks):
    q, s = dist.shape
    blocks = ATTN_GROUP // 2
    order = lambda a: np.transpose(a.reshape((ATTN_KV_HEADS, blocks, 2) + a.shape[1:]), (0, 2, 1, 3, 4))
    base = np.where(valid[None], -_alibi_slopes()[:, None, None] * dist[None], -np.inf)
    base = order(base).reshape(ATTN_KV_HEADS, 2, blocks * q, s).astype(np.float32)
    sink = jnp.transpose(sinks.astype(F32).reshape(ATTN_KV_HEADS, blocks, 2), (0, 2, 1))
    sink = jnp.broadcast_to(sink[:, :, :, None, None], (ATTN_KV_HEADS, 2, blocks, q, 1))
    sink = sink.reshape(ATTN_KV_HEADS, 2, blocks * q, 1)
    tail = jnp.full((ATTN_KV_HEADS, 2, blocks * q, ATTN_KEY_PAD - s - 1), -jnp.inf, F32)
    return jnp.concatenate([jnp.asarray(base), sink, tail], axis=-1)


def _attn_block(q_ref, kvc_ref, kvp_ref, bias_ref, n_chunks):
    i = pl.program_id(1)
    kv = jnp.concatenate([kvp_ref[...], kvc_ref[...]], axis=0)
    k_var = _kv_variants(kv[:, :ATTN_KV_DIM])
    v_var = _kv_variants(kv[:, ATTN_KV_DIM:])
    span = (WINDOW_CHUNKS + 1) * CHUNK
    outs = []
    for cc in range(n_chunks):
        variant = jnp.minimum(i * n_chunks + cc, WINDOW_CHUNKS)
        band = slice(cc * CHUNK, cc * CHUNK + span)
        outs.append(_attn_rows(
            q_ref[pl.ds(cc * CHUNK, CHUNK), :], {key: t[band] for key, t in k_var.items()},
            {key: t[band] for key, t in v_var.items()}, lambda h, half: bias_ref[variant, h, half]))
    return outs


def _attn_prompt_kernel(q_ref, kvc_ref, kvp_ref, bias_ref, o_ref, *, n_chunks):
    for cc, rows in enumerate(_attn_block(q_ref, kvc_ref, kvp_ref, bias_ref, n_chunks)):
        o_ref[pl.ds(cc * CHUNK, CHUNK), :] = rows.astype(o_ref.dtype)


def attn_prompt(pa, pb, sinks, *, qb, out_dtype=BF16):
    b, s, _ = pa.shape
    span = (WINDOW_CHUNKS + 1) * CHUNK
    qi = np.arange(CHUNK)
    kj = np.arange(span)
    dist = np.abs(qi[:, None] + WINDOW_CHUNKS * CHUNK - kj[None, :]).astype(np.float64)
    bias = jnp.stack([
        _attn_bias(dist, np.broadcast_to(c - WINDOW_CHUNKS + kj[None, :] // CHUNK >= 0, dist.shape), sinks)
        for c in range(WINDOW_CHUNKS + 1)])
    kv_w = 2 * ATTN_KV_DIM
    return pl.pallas_call(
        functools.partial(_attn_prompt_kernel, n_chunks=qb // CHUNK),
        grid=(b, s // qb),
        in_specs=[
            pl.BlockSpec((None, qb, ATTN_Q_DIM), lambda b, i: (b, i, PA_Q // ATTN_Q_DIM)),
            pl.BlockSpec((None, qb, kv_w), lambda b, i: (b, i, PB_K // kv_w)),
            pl.BlockSpec((None, WINDOW, kv_w), lambda b, i: (b, jnp.maximum(i * (qb // WINDOW) - 1, 0), PB_K // kv_w)),
            pl.BlockSpec(bias.shape, lambda b, i: (0,) * bias.ndim),
        ],
        out_specs=pl.BlockSpec((None, qb, ATTN_Q_DIM), lambda b, i: (b, i, 0)),
        out_shape=jax.ShapeDtypeStruct((b, s, ATTN_Q_DIM), out_dtype),
        compiler_params=_params("arbitrary", "arbitrary"),
        name="attn_prompt",
    )(pa, pb, pb, bias)


def _attn_sample_kernel(q_ref, kvn_ref, kc_ref, vc_ref, bias_ref, o_ref):
    kvn = kvn_ref[...]
    k_var = _kv_variants(jnp.concatenate([kc_ref[...], kvn[:, :ATTN_KV_DIM]], axis=0))
    v_var = _kv_variants(jnp.concatenate([vc_ref[...], kvn[:, ATTN_KV_DIM:]], axis=0))
    o_ref[...] = _attn_rows(q_ref[...], k_var, v_var, lambda h, half: bias_ref[h, half]).astype(o_ref.dtype)


def attn_sample(pa, pb, k_cache, v_cache, sinks, *, out_dtype=BF16):
    b, length, _ = pa.shape
    rows = k_cache.shape[1]
    qpos = PAST_LEN + np.arange(length)
    kpos = PAST_LEN - rows + np.arange(rows + length)
    qch, kch = qpos // CHUNK, kpos // CHUNK
    valid = (kch[None, :] <= qch[:, None]) & (kch[None, :] >= qch[:, None] - WINDOW_CHUNKS)
    dist = np.abs(qpos[:, None] - kpos[None, :]).astype(np.float64)
    bias = _attn_bias(dist, valid, sinks)
    kv_w = 2 * ATTN_KV_DIM
    cache = pl.BlockSpec((None, rows, ATTN_KV_DIM), lambda b: (b, 0, 0))
    return pl.pallas_call(
        _attn_sample_kernel,
        grid=(b,),
        in_specs=[
            pl.BlockSpec((None, length, ATTN_Q_DIM), lambda b: (b, 0, PA_Q // ATTN_Q_DIM)),
            pl.BlockSpec((None, length, kv_w), lambda b: (b, 0, PB_K // kv_w)),
            cache, cache,
            pl.BlockSpec(bias.shape, lambda b: (0,) * bias.ndim),
        ],
        out_specs=pl.BlockSpec((None, length, ATTN_Q_DIM), lambda b: (b, 0, 0)),
        out_shape=jax.ShapeDtypeStruct((b, length, ATTN_Q_DIM), out_dtype),
        compiler_params=_params("arbitrary"),
        name="attn_sample",
    )(pa, pb, k_cache, v_cache, bias)


def _causal_conv(tail_ref, prev_ref, cur_b, shift_ref, w_ref, b_ref, first):
    k = w_ref.shape[0]
    q = cur_b.shape[0]

    @pl.when(first)
    def _():
        tail_ref[pl.ds(0, SUBLANES), :] = prev_ref[...]

    cur = cur_b.astype(F32)
    tail_ref[pl.ds(SUBLANES, SUBLANES), :] = cur[:SUBLANES]
    acc = b_ref[...] + w_ref[k - 1:k, :] * cur
    for t in range(k - 1):
        shift = k - 1 - t
        moved = jnp.dot(shift_ref[shift - 1], cur_b, preferred_element_type=F32)
        head = tail_ref[pl.ds(SUBLANES - shift, SUBLANES), :]
        acc = acc + w_ref[t:t + 1, :] * jnp.concatenate([head, moved[SUBLANES:]], axis=0)
    tail_ref[pl.ds(0, SUBLANES), :] = cur[q - SUBLANES:]
    return acc


def _split3(x):
    hi = x.astype(BF16)
    r = x - hi.astype(F32)
    mid = r.astype(BF16)
    lo = (r - mid.astype(F32)).astype(BF16)
    return hi, mid, lo


def _ssd_block(x_ref, bc_ref, z_ref, dt_ref, xprev_ref, bcprev_ref, h0_ref, shift_ref,
               wx_ref, bx_ref, wbc_ref, bbc_ref, dtb_ref, alog_ref, dskip_ref, ng_ref,
               hout_ref, xpad, bcpad, h_scr, *, q, n_blocks, valid_len):
    i = pl.program_id(1)
    first = i == 0

    @pl.when(first)
    def _():
        h_scr[...] = h0_ref[...]

    xs = _silu(_causal_conv(xpad, xprev_ref, x_ref[...], shift_ref, wx_ref, bx_ref, first))
    bc = _silu(_causal_conv(bcpad, bcprev_ref, bc_ref[...], shift_ref, wbc_ref, bbc_ref, first))
    xs_b = xs.astype(BF16)

    dt = _softplus(dt_ref[...] + dtb_ref[...])
    if valid_len < q:
        row = lax.broadcasted_iota(jnp.int32, dt.shape, 0)
        dt = jnp.where(row < valid_len, dt, 0.0)
    da = dt * (-jnp.exp(alog_ref[...]))
    r_io = lax.broadcasted_iota(jnp.int32, (q, q), 0)
    c_io = lax.broadcasted_iota(jnp.int32, (q, q), 1)
    causal = c_io <= r_io
    tri = jnp.where(causal, 1.0, 0.0).astype(BF16)
    cum = sum(jnp.dot(tri, part, preferred_element_type=F32) for part in _split3(da))
    cum_t = cum.T
    dt_t = dt.T
    lane = lax.broadcasted_iota(jnp.int32, (1, LANES), 1)
    low_half = lane < SSD_HEAD_DIM

    y_pairs = []
    for g in range(SSD_GROUPS):
        b_g = bc[:, g * SSD_STATE:(g + 1) * SSD_STATE]
        c_g = bc[:, (SSD_GROUPS + g) * SSD_STATE:(SSD_GROUPS + g + 1) * SSD_STATE]
        cb = lax.dot_general(c_g.astype(BF16), b_g.astype(BF16), (((1,), (1,)), ((), ())),
                             preferred_element_type=F32)
        b_t = b_g.T
        for pr in range(g * SSD_HPG // 2, (g + 1) * SSD_HPG // 2):
            x_pair = xs_b[:, pr * LANES:(pr + 1) * LANES]
            h_pair = h_scr[pr]
            h_pair_b = h_pair.astype(BF16)
            ys, states, decays = [], [], []
            for j in (2 * pr, 2 * pr + 1):
                col = jnp.broadcast_to(cum[:, j:j + 1], (q, q))
                row = cum_t[j:j + 1, :]
                decay_in = jnp.exp(jnp.where(causal, col - row, -jnp.inf))
                m = (cb * decay_in * dt_t[j:j + 1, :]).astype(BF16)
                c_scaled = (c_g * jnp.exp(col[:, :SSD_STATE])).astype(BF16)
                ys.append(jnp.dot(m, x_pair, preferred_element_type=F32)
                          + jnp.dot(c_scaled, h_pair_b, preferred_element_type=F32))
                last = row[:, q - 1:q]
                w_row = dt_t[j:j + 1, :] * jnp.exp(last - row)
                states.append(jnp.dot((b_t * w_row).astype(BF16), x_pair, preferred_element_type=F32))
                decays.append(jnp.exp(last))
            h_scr[pr] = (jnp.where(low_half, decays[0], decays[1]) * h_pair
                         + jnp.where(low_half, states[0], states[1]))
            y_pairs.append(jnp.where(low_half, ys[0], ys[1]))
    y = jnp.concatenate(y_pairs, axis=1) + dskip_ref[...] * xs
    y = y * _silu(z_ref[...].astype(F32))
    half = SSD_INNER // SSD_GROUPS
    normed = []
    for g in range(SSD_GROUPS):
        yg = y[:, g * half:(g + 1) * half]
        normed.append(yg * lax.rsqrt(jnp.mean(yg * yg, axis=-1, keepdims=True) + EPS))
    @pl.when(i == n_blocks - 1)
    def _():
        hout_ref[...] = h_scr[...]

    return jnp.concatenate(normed, axis=1) * ng_ref[...]


def _ssd_kernel(*refs, **static):
    y_ref = refs[16]
    y_ref[...] = _ssd_block(*refs[:16], *refs[17:], **static).astype(y_ref.dtype)


def ssd_mixer(pa, pb, x_prev, bc_prev, h0, conv_w, conv_b, dt_bias, a_log, d_skip, norm_g, *, q, valid_len=None,
              out_dtype=BF16):
    b, s, _ = pa.shape
    valid_len = q if valid_len is None else valid_len
    parts = _ssd_parts(pa, pb, x_prev, bc_prev, h0, conv_w, conv_b, dt_bias, a_log, d_skip, norm_g, q)
    return pl.pallas_call(
        functools.partial(_ssd_kernel, q=q, n_blocks=s // q, valid_len=valid_len),
        grid=(b, s // q),
        in_specs=parts.in_specs,
        out_specs=[pl.BlockSpec((None, q, SSD_INNER), lambda b, i: (b, i, 0)), parts.state_spec],
        out_shape=[jax.ShapeDtypeStruct((b, s, SSD_INNER), out_dtype), parts.state_shape],
        scratch_shapes=parts.scratch,
        compiler_params=_params("arbitrary", "arbitrary"),
        name="ssd_mixer",
    )(*parts.args)


class _SsdParts(NamedTuple):
    in_specs: list
    args: list
    scratch: list
    state_spec: Any
    state_shape: Any


def _ssd_parts(pa, pb, x_prev, bc_prev, h0, conv_w, conv_b, dt_bias, a_log, d_skip, norm_g, q):
    b = pa.shape[0]
    pad128 = lambda v: jnp.pad(v.astype(F32), (0, LANES - v.shape[0])).reshape(1, LANES)
    col = lambda width, off: pl.BlockSpec((None, q, width), lambda b, i: (b, i, off // width))
    const = lambda shape: pl.BlockSpec(shape, lambda b, i: (0,) * len(shape))
    per_b = lambda shape: pl.BlockSpec((None,) + shape, lambda b, i: (b,) + (0,) * len(shape))
    h_shape = (SSD_HEADS // 2, SSD_STATE, LANES)
    shifts = np.stack([np.eye(q, k=-s, dtype=np.float32) for s in range(1, SSD_CONV)])
    in_specs = [
        col(SSD_INNER, PA_X), col(SSD_BC_DIM, PA_BC), col(SSD_INNER, PA_Z), col(LANES, PB_DT),
        per_b((SUBLANES, SSD_INNER)), per_b((SUBLANES, SSD_BC_DIM)), per_b(h_shape),
        const((SSD_CONV - 1, q, q)), const((SSD_CONV, SSD_INNER)), const((1, SSD_INNER)),
        const((SSD_CONV, SSD_BC_DIM)), const((1, SSD_BC_DIM)),
        const((1, LANES)), const((1, LANES)), const((1, SSD_INNER)), const((1, SSD_INNER)),
    ]
    args = [pa, pa, pa, pb, x_prev, bc_prev, h0, jnp.asarray(shifts, BF16),
            conv_w[:, :SSD_INNER], conv_b[:SSD_INNER].reshape(1, -1),
            conv_w[:, SSD_INNER:], conv_b[SSD_INNER:].reshape(1, -1),
            pad128(dt_bias), pad128(a_log),
            jnp.repeat(d_skip.astype(F32), SSD_HEAD_DIM).reshape(1, -1), norm_g.astype(F32).reshape(1, -1)]
    scratch = [pltpu.VMEM((2 * SUBLANES, SSD_INNER), F32), pltpu.VMEM((2 * SUBLANES, SSD_BC_DIM), F32),
               pltpu.VMEM(h_shape, F32)]
    return _SsdParts(in_specs, args, scratch, per_b(h_shape), jax.ShapeDtypeStruct((b,) + h_shape, F32))


def _pair_state(h):
    b = h.shape[0]
    h = h.astype(F32).reshape(b, SSD_HEADS // 2, 2, SSD_HEAD_DIM, SSD_STATE)
    return h.transpose(0, 1, 4, 2, 3).reshape(b, SSD_HEADS // 2, SSD_STATE, 2 * SSD_HEAD_DIM)


def _unpair_state(h):
    b = h.shape[0]
    h = h.reshape(b, SSD_HEADS // 2, SSD_STATE, 2, SSD_HEAD_DIM)
    return h.transpose(0, 1, 3, 4, 2).reshape(b, SSD_HEADS, SSD_HEAD_DIM, SSD_STATE)


def _tail_rows(prev):
    return jnp.pad(prev.astype(F32), ((0, 0), (SUBLANES - prev.shape[1], 0), (0, 0)))


def _outproj_kernel(x_ref, a_ref, y_ref, gt_ref, wa_ref, wy_ref, o_ref):
    m = (jnp.dot(a_ref[...].astype(BF16), wa_ref[...], preferred_element_type=F32)
         + jnp.dot(y_ref[...].astype(BF16), wy_ref[...], preferred_element_type=F32))
    o_ref[...] = x_ref[...] + gt_ref[...] * m


def mix0_out(x, attn, y, gt, w_out, *, tm):
    b, s, d = x.shape
    da, dy = attn.shape[2], y.shape[2]
    mod_rows = 1 if gt.shape[1] == 1 else tm
    x_spec = pl.BlockSpec((None, tm, d), lambda b, i: (b, i, 0))
    return pl.pallas_call(
        _outproj_kernel,
        grid=(b, s // tm),
        in_specs=[
            x_spec,
            pl.BlockSpec((None, tm, da), lambda b, i: (b, i, 0)),
            pl.BlockSpec((None, tm, dy), lambda b, i: (b, i, 0)),
            _mod_spec(mod_rows, d),
            pl.BlockSpec((da, d), lambda b, i: (0, 0)),
            pl.BlockSpec((dy, d), lambda b, i: (1, 0)),
        ],
        out_specs=x_spec,
        out_shape=jax.ShapeDtypeStruct(x.shape, F32),
        compiler_params=_params("arbitrary", "arbitrary"),
        name="mix0_out",
    )(x, attn, y, gt, w_out, w_out)


SCONV_TC = 512


def _sconv_kernel(x_ref, xn_ref, g_ref, sh_ref, sc_ref, gt_ref, prev_ref, cw_ref, win_ref, wout_ref,
                  o_ref, buf_ref, h_even, h_odd, rs_scr, vpad, carry, *, tm):
    b, i = pl.program_id(0), pl.program_id(1)
    n_b, n_t = pl.num_programs(0), pl.num_programs(1)
    tile = b * n_t + i
    bn, _ = _next_tile(b, i, n_b, n_t)
    d = x_ref.shape[1]
    k = cw_ref.shape[0]
    tc = SCONV_TC
    n_c = d // tc

    @pl.when(i == 0)
    def _():
        carry[...] = prev_ref[...]

    @pl.when(tile == 0)
    def _():
        _modnorm_store(h_even, x_ref, g_ref, sc_ref.at[b], sh_ref.at[b], rs_scr)

    def step(h_cur, h_nxt):
        tie = _modnorm_rows(h_nxt, 0, xn_ref, g_ref, sc_ref.at[bn], sh_ref.at[bn])
        h = h_cur[...]

        def in_proj(c):
            return [jnp.dot(h, win_ref[:, part * d + c * tc:part * d + (c + 1) * tc], preferred_element_type=F32)
                    for part in (1, 2, 0)]

        def mix(c, gate_c, xi, gate_b):
            cols = slice(c * tc, (c + 1) * tc)
            v = gate_c * xi
            vpad[pl.ds(0, SUBLANES), cols] = carry[:, cols]
            vpad[pl.ds(SUBLANES, tm), cols] = v
            u = cw_ref[k - 1:k, cols] * v
            for t in range(k - 1):
                u = u + cw_ref[t:t + 1, cols] * vpad[pl.ds(SUBLANES - (k - 1) + t, tm), cols]
            carry[:, cols] = vpad[pl.ds(tm, SUBLANES), cols]
            return gate_b * u

        acc = None
        pending = in_proj(0)
        for c in range(n_c):
            following = in_proj(c + 1) if c + 1 < n_c else None
            r = mix(c, *pending)
            if c == 0:
                r = _tied(r, tie)
            part = jnp.dot(r.astype(BF16), wout_ref[c * tc:(c + 1) * tc, :], preferred_element_type=F32)
            acc = part if acc is None else acc + part
            pending = following
        o_ref[...] = x_ref[...] + gt_ref[b] * acc
        buf_ref[...] = carry[...]

    pl.when(tile % 2 == 0)(functools.partial(step, h_even, h_odd))
    pl.when(tile % 2 == 1)(functools.partial(step, h_odd, h_even))


def sconv_mixer(x, g, sh, sc, gt, prev, conv_w, w_in, w_out, *, tm):
    b, s, d = x.shape
    n_t = s // tm
    x_spec = pl.BlockSpec((None, tm, d), lambda b, i: (b, i, 0))
    next_spec = pl.BlockSpec((None, tm, d), lambda bb, i: _next_tile(bb, i, b, n_t) + (0,))
    buf_spec = pl.BlockSpec((None, SUBLANES, d), lambda b, i: (b, 0, 0))
    const = lambda shape: pl.BlockSpec(shape, lambda b, i: (0,) * len(shape))
    resident = lambda shape: pl.BlockSpec(shape, lambda b, i: (0, 0), pipeline_mode=pl.Buffered(1))
    return pl.pallas_call(
        functools.partial(_sconv_kernel, tm=tm),
        grid=(b, n_t),
        in_specs=[
            x_spec, next_spec, const((1, d)), const(sh.shape), const(sc.shape), const(gt.shape), buf_spec,
            const((SCONV_WIDTH, d)), resident((d, 3 * d)), resident((d, d)),
        ],
        out_specs=[x_spec, buf_spec],
        out_shape=[jax.ShapeDtypeStruct(x.shape, F32), jax.ShapeDtypeStruct((b, SUBLANES, d), F32)],
        scratch_shapes=[
            pltpu.VMEM((tm, d), BF16), pltpu.VMEM((tm, d), BF16), pltpu.VMEM((tm, LANES), F32),
            pltpu.VMEM((tm + SUBLANES, d), F32), pltpu.VMEM((SUBLANES, d), F32),
        ],
        compiler_params=_params("arbitrary", "arbitrary"),
        name="sconv_mixer",
    )(x, x, g.reshape(1, d), sh, sc, gt, prev, conv_w.astype(F32), w_in, w_out)


def _arrange_w_in0(w):
    i1 = ATTN_Q_DIM
    i2 = i1 + ATTN_KV_DIM
    i3 = i2 + ATTN_KV_DIM
    i4 = i3 + SSD_INNER
    i5 = i4 + SSD_CONV_DIM
    q, k, v, z, xbc, dt = (w[:, a:b] for a, b in ((0, i1), (i1, i2), (i2, i3), (i3, i4), (i4, i5), (i5, w.shape[1])))
    pad = jnp.zeros((w.shape[0], PB_WIDTH - PB_DT - dt.shape[1]), w.dtype)
    return jnp.concatenate([q, z, xbc, k, v, dt, pad], axis=1).astype(BF16)


def _split_mod(mod):
    return [m[:, None, :] for m in jnp.split(mod, N_MOD, axis=-1)]


def _per_token(m, length):
    b, _, d = m.shape
    return jnp.broadcast_to(m, (b, length, d)).reshape(1, b * length, d)


class _Tiles(NamedTuple):
    ffn: int
    proj: int
    mix_out: int
    sconv: int
    ssd: int
    attn: int


def _tiles(batch, seq, flat):
    if flat:
        rows = batch * seq
        return _Tiles(ffn=rows, proj=rows, mix_out=rows, sconv=seq, ssd=LANES, attn=seq)
    return _Tiles(ffn=min(1024, seq), proj=min(512, seq), mix_out=min(1024, seq), sconv=min(256, seq),
                  ssd=min(256, seq), attn=min(1024, seq))


def _trunk(x, mods, weights, cache):
    b, s, d = x.shape
    t = _tiles(b, s, cache is not None)
    tm, tm_proj, tm_mix, q_ssd, qb_attn = t.ffn, t.proj, t.mix_out, t.ssd, t.attn
    flat = cache is not None
    if flat:
        as_rows = lambda t: t.reshape(1, b * s, t.shape[-1])
        mod_of = lambda m: _per_token(m, s)
    else:
        as_rows = lambda t: t
        mod_of = lambda m: m
    unrow = lambda t: t.reshape(b, s, t.shape[-1])
    w = weights

    def ffn(xr, l, k, sh, sc, gt, final_g=None):
        return ffn_half(xr, w["norm_g"][l, 2 * k], mod_of(sh), mod_of(sc), mod_of(gt),
                        w["ffn_gate"], w["ffn_up"], w["ffn_down"], l, k, final_g, tm=tm)

    sh1, sc1, g1, sh2, sc2, g2, sh3, sc3, g3 = mods[0]
    xr = ffn(as_rows(x), 0, 0, sh1, sc1, g1)
    pa, pb = modnorm_proj(xr, w["norm_g"][0, 1], mod_of(sh2), mod_of(sc2), w["w_in0"], tm=tm_proj,
                          tail_width=PB_WIDTH)
    pa, pb = unrow(pa), unrow(pb)
    if cache is None:
        x_prev = jnp.zeros((b, SUBLANES, SSD_INNER), F32)
        bc_prev = jnp.zeros((b, SUBLANES, SSD_BC_DIM), F32)
        h0 = jnp.zeros((b, SSD_HEADS // 2, SSD_STATE, LANES), F32)
        attn = attn_prompt(pa, pb, w["sinks"], qb=qb_attn)
        y, h_new = ssd_mixer(pa, pb, x_prev, bc_prev, h0, *w["ssd"], q=q_ssd)
        xr = mix0_out(xr, attn, y, g2, w["w_out0"], tm=tm_mix)
        new_kv = pb[:, s - WINDOW:]
        sconv_prev = jnp.zeros((b, SUBLANES, d), F32)
    else:
        k_cache, v_cache, h_state, conv_prev, sconv_state = cache
        rows = k_cache.shape[1]
        attn = attn_sample(pa, pb, k_cache.reshape(b, rows, ATTN_KV_DIM), v_cache.reshape(b, rows, ATTN_KV_DIM),
                           w["sinks"])
        tail = _tail_rows(conv_prev)
        pad_rows = lambda t: jnp.pad(t, ((0, 0), (0, q_ssd - s), (0, 0)))
        y, h_new = ssd_mixer(pad_rows(pa), pad_rows(pb), tail[:, :, :SSD_INNER], tail[:, :, SSD_INNER:],
                             _pair_state(h_state), *w["ssd"], q=q_ssd, valid_len=s)
        xr = mix0_out(xr, as_rows(attn), as_rows(y[:, :s]), mod_of(g2), w["w_out0"], tm=tm_mix)
        new_kv = pb
        sconv_prev = _tail_rows(sconv_state)
    xbc_raw = pa[:, s - (SSD_CONV - 1):, PA_X:PA_X + SSD_CONV_DIM].astype(F32)
    kv_shape = (b, new_kv.shape[1], ATTN_KV_HEADS, HEAD_DIM)
    states0 = (new_kv[:, :, PB_K:PB_K + ATTN_KV_DIM].reshape(kv_shape),
               new_kv[:, :, PB_V:PB_V + ATTN_KV_DIM].reshape(kv_shape), _unpair_state(h_new), xbc_raw)
    xr = ffn(xr, 0, 1, sh3, sc3, g3)

    sh1, sc1, g1, sh2, sc2, g2, sh3, sc3, g3 = mods[1]
    xr = ffn(xr, 1, 0, sh1, sc1, g1)
    xm, buf = sconv_mixer(unrow(xr), w["norm_g"][1, 1], sh2, sc2, g2, sconv_prev, w["sconv_w"], w["w_in1"],
                          w["w_out1"], tm=t.sconv)
    sconv_new = buf[:, SUBLANES - (SCONV_WIDTH - 1):]
    y_out = unrow(ffn(as_rows(xm), 1, 1, sh3, sc3, g3, final_g=w["final_g"]))
    return y_out, states0, sconv_new


def kernel(x_prompt, x_sample, c_prompt, c_sample, cache_swa_k, cache_swa_v, state_ssd, state_ssd_conv, state_sconv, norm_g, w_ada, b_ada, w_ffn_gate, w_ffn_up, w_ffn_down, w_in_mix0, w_out_mix0, attn_sinks, ssd_conv_w, ssd_conv_b, ssd_dt_bias, ssd_a_log, ssd_d, ssd_norm_g, w_in_mix1, sconv_w, w_out_mix1, final_norm_g):
    bp, bs = c_prompt.shape[0], c_sample.shape[0]
    c_all = jnp.concatenate([c_prompt, c_sample], axis=0)
    c_rows = -(-c_all.shape[0] // SUBLANES) * SUBLANES
    c_all = jnp.pad(c_all, ((0, c_rows - c_all.shape[0]), (0, 0)))
    mod = ada_modulation(c_all, w_ada, b_ada)
    mods_p = [_split_mod(mod[l, :bp]) for l in range(mod.shape[0])]
    mods_s = [_split_mod(mod[l, bp:bp + bs]) for l in range(mod.shape[0])]

    weights = {
        "norm_g": norm_g,
        "ffn_gate": w_ffn_gate.astype(BF16), "ffn_up": w_ffn_up.astype(BF16), "ffn_down": w_ffn_down.astype(BF16),
        "w_in0": _arrange_w_in0(w_in_mix0[0]), "w_out0": w_out_mix0[0].astype(BF16),
        "sinks": attn_sinks[0],
        "ssd": (ssd_conv_w[0].astype(F32), ssd_conv_b[0].astype(F32), ssd_dt_bias[0], ssd_a_log[0], ssd_d[0],
                ssd_norm_g[0]),
        "w_in1": w_in_mix1[0].astype(BF16), "sconv_w": sconv_w[0], "w_out1": w_out_mix1[0].astype(BF16),
        "final_g": final_norm_g,
    }
    y_p, (k_p, v_p, h_p, cv_p), sc_p = _trunk(x_prompt, mods_p, weights, None)
    cache = (cache_swa_k[0], cache_swa_v[0], state_ssd[0], state_ssd_conv[0], state_sconv[0])
    y_s, (k_s, v_s, h_s, cv_s), sc_s = _trunk(x_sample, mods_s, weights, cache)
    stack = lambda t: t[None]
    return (y_p, y_s, stack(k_p), stack(v_p), stack(h_p), stack(cv_p), stack(sc_p),
            stack(k_s), stack(v_s), stack(h_s), stack(cv_s), stack(sc_s))
```
